```python
import jax, jax.numpy as jnp
from jax import lax
import numpy as np


D_MODEL = 4096
BATCH = 1
SEQ = 16384
DEPTH = 4

CHUNK = 64
N_MEM = 256
D_FF = 2048
EPS = 1e-6

SB_HEADS = 4
SB_DH = 256
SB_W = SB_HEADS * SB_DH
SB_QBLOCK = 128

GLA_HEADS = 4
GLA_DK = 128
GLA_DV = 256
GLA_KW = GLA_HEADS * GLA_DK
GLA_VW = GLA_HEADS * GLA_DV
GLA_GATE_RANK = 16
GLA_TAU = 16.0

RW_HEADS = 16
RW_DH = 64
RW_W = RW_HEADS * RW_DH
RW_DECAY_RANK = 128
RW_A_RANK = 128
RW_V_RANK = 96
RW_G_RANK = 480
RW_LN_EPS = 64e-5
RW_CHUNK_LOG2 = 4
RW_CHUNK = 2 ** RW_CHUNK_LOG2

XA_HEADS = 4
XA_DH = 256
XA_W = XA_HEADS * XA_DH

N_BRANCH = 3
GATE_RANK = 512
IN_WIDTHS = (SB_W, SB_W, SB_W, GLA_KW, GLA_KW, GLA_VW, GLA_VW, RW_W, RW_W, RW_W)
D_IN = sum(IN_WIDTHS)
SPLIT_POINTS = tuple(int(c) for c in np.cumsum(IN_WIDTHS)[:-1])

kernel_name = 'hybrid_sb_gla_rwkv7_streaming_encoder'


def rms_norm(x, g):
    xf = x.astype(jnp.float32)
    y = xf * lax.rsqrt(jnp.mean(xf * xf, axis=-1, keepdims=True) + EPS)
    return (y * g.astype(jnp.float32)).astype(x.dtype)


def time_shift(x):
    return jnp.pad(x, ((0, 0), (1, 0), (0, 0)))[:, :-1]


def swiglu(h, w1, w3, w2):
    return (jax.nn.silu(h @ w1) * (h @ w3)) @ w2


def stick_breaking_attention(q, k, v):
    f32 = jnp.float32
    B, T, _ = q.shape
    Q = SB_QBLOCK
    nb = T // Q
    qh = q.reshape(B, T, SB_HEADS, SB_DH).transpose(0, 2, 1, 3)
    kh = k.reshape(B, T, SB_HEADS, SB_DH).transpose(0, 2, 1, 3)
    vh = v.reshape(B, T, SB_HEADS, SB_DH).transpose(0, 2, 1, 3)
    scale = SB_DH ** -0.5
    tri = jnp.tril(jnp.ones((Q, Q), f32))
    q_pos = jnp.arange(Q)
    outs = []
    for i in range(nb):
        n = (i + 1) * Q
        z = jnp.einsum('bhqd,bhsd->bhqs', qh[:, :, i * Q:n], kh[:, :, :n]).astype(f32) * scale
        causal = jnp.arange(n)[None, :] < (i * Q + q_pos)[:, None]
        log_keep = jnp.where(causal, jax.nn.log_sigmoid(-z), 0.0).reshape(B, SB_HEADS, Q, i + 1, Q)
        suffix = jnp.einsum('bhqnj,js->bhqns', log_keep, tri)
        blk = suffix[..., 0]
        carry = lax.cumsum(blk, axis=3, reverse=True) - blk
        later = (suffix - log_keep + carry[..., None]).reshape(B, SB_HEADS, Q, n)
        a = jnp.where(causal, jnp.exp(jax.nn.log_sigmoid(z) + later), 0.0)
        outs.append(jnp.einsum('bhqs,bhsd->bhqd', a.astype(vh.dtype), vh[:, :, :n]))
    o = jnp.concatenate(outs, axis=2)
    return o.transpose(0, 2, 1, 3).reshape(B, T, SB_W)


def gla_attention(q, k, v, g, log_alpha, norm_g):
    f32 = jnp.float32
    B, T, _ = q.shape
    nc = T // CHUNK

    def heads(x, d):
        return x.reshape(B, nc, CHUNK, GLA_HEADS, d).transpose(0, 3, 1, 2, 4).astype(f32)

    qh = heads(q, GLA_DK) * (GLA_DK ** -0.5)
    kh = heads(k, GLA_DK)
    vh = heads(v, GLA_DV)
    b = jnp.cumsum(heads(log_alpha, GLA_DK), axis=3)
    b_last = b[:, :, :, -1:, :]
    q_in = qh * jnp.exp(b)
    k_in = kh * jnp.exp(-b)
    causal = jnp.tril(jnp.ones((CHUNK, CHUNK), bool))
    scores = jnp.where(causal, jnp.einsum('bhnqd,bhnsd->bhnqs', q_in, k_in), 0.0)
    o_intra = jnp.einsum('bhnqs,bhnsv->bhnqv', scores, vh)
    kv = jnp.einsum('bhnsd,bhnsv->bhndv', kh * jnp.exp(b_last - b), vh)
    decay = jnp.exp(b_last[:, :, :, 0, :])

    def step(S, inp):
        dec_n, kv_n = inp
        return dec_n[..., None] * S + kv_n, S

    S0 = jnp.zeros((B, GLA_HEADS, GLA_DK, GLA_DV), f32)
    _, S_prev = lax.scan(step, S0, (jnp.moveaxis(decay, 2, 0), jnp.moveaxis(kv, 2, 0)))
    S_prev = jnp.moveaxis(S_prev, 0, 2)
    o = o_intra + jnp.einsum('bhnqd,bhndv->bhnqv', q_in, S_prev)
    o = o * lax.rsqrt(jnp.mean(o * o, axis=-1, keepdims=True) + EPS) * norm_g.astype(f32)
    o = o.transpose(0, 2, 3, 1, 4).reshape(B, T, GLA_VW)
    return (o * jax.nn.silu(g.astype(f32))).astype(v.dtype)


def wkv7_chunked(r, log_w, k, v, a_vec, b_vec):
    f32 = jnp.float32
    B, T, H, N = r.shape
    C = RW_CHUNK
    nc = T // C

    def ch(t):
        return t.reshape(B, nc, C, H, N).transpose(0, 3, 1, 2, 4)

    r, log_w, k, v, a_vec, b_vec = (ch(t) for t in (r, log_w, k, v, a_vec, b_vec))
    g_in = jnp.cumsum(log_w, axis=3)
    g_ex = g_in - log_w
    g_last = g_in[:, :, :, -1:, :]
    r_d = r * jnp.exp(g_in)
    a_d = a_vec * jnp.exp(g_ex)
    inv = jnp.exp(-g_in)
    b_h = b_vec * inv
    k_h = k * inv
    to_end = jnp.exp(g_last - g_in)
    b_e = b_vec * to_end
    k_e = k * to_end
    strict = jnp.tril(jnp.ones((C, C), bool), -1)
    incl = jnp.tril(jnp.ones((C, C), bool))

    def gram(x, y, m):
        return jnp.where(m, jnp.einsum('bhntd,bhnsd->bhnts', x, y), 0.0)

    L_ab = gram(a_d, b_h, strict)
    L_ak = gram(a_d, k_h, strict)
    M_rb = gram(r_d, b_h, incl)
    M_rk = gram(r_d, k_h, incl)
    inv_m = jnp.eye(C, dtype=f32) + L_ab
    p = L_ab
    for _ in range(RW_CHUNK_LOG2 - 1):
        p = p @ p
        inv_m = inv_m + inv_m @ p
    w_a = inv_m @ a_d
    u_v = inv_m @ (L_ak @ v)
    o_loc = M_rk @ v
    dec_end = jnp.exp(g_last[:, :, :, 0, :])

    def step(S, inp):
        w_a_n, u_v_n, r_n, m_rb_n, o_n, b_e_n, k_e_n, v_n, dec_n = inp
        u = w_a_n @ S + u_v_n
        o = r_n @ S + m_rb_n @ u + o_n
        S = dec_n[..., None] * S + jnp.swapaxes(b_e_n, -1, -2) @ u + jnp.swapaxes(k_e_n, -1, -2) @ v_n
        return S, o

    xs = tuple(jnp.moveaxis(t, 2, 0) for t in (w_a, u_v, r_d, M_rb, o_loc, b_e, k_e, v, dec_end))
    _, o = lax.scan(step, jnp.zeros((B, H, N, N), f32), xs)
    return o.transpose(1, 0, 3, 2, 4).reshape(B, T, H, N)


def rwkv7_time_mix(h, r_p, k_p, v_p, v_first, vres, mu_rkv, mu_wag, w0, w1, w2, a0, a1, a2,
                   g1, g2, k_k, k_a, r_k, ln_g, ln_b):
    f32 = jnp.float32
    B, T, _ = h.shape
    dh = time_shift(h) - h
    xw = h + dh * mu_wag[0]
    xa = h + dh * mu_wag[1]
    xg = h + dh * mu_wag[2]
    r = r_p + (time_shift(r_p) - r_p) * mu_rkv[0]
    k = k_p + (time_shift(k_p) - k_p) * mu_rkv[1]
    v = v_p + (time_shift(v_p) - v_p) * mu_rkv[2]
    w_log = -jax.nn.softplus(-(w0 + jnp.tanh(xw @ w1) @ w2)) - 0.5
    log_decay = -jnp.exp(w_log.astype(f32))
    if vres is None:
        v_first = v
    else:
        mu_vl, v0, v1, v2 = vres
        xv = h + dh * mu_vl
        v = v + (v_first - v) * jax.nn.sigmoid(v0 + (xv @ v1) @ v2)
    a = jax.nn.sigmoid(a0 + (xa @ a1) @ a2)
    g = jax.nn.sigmoid(xg @ g1) @ g2

    def hd(t):
        return t.reshape(B, T, RW_HEADS, RW_DH).astype(f32)

    kk = hd(k * k_k)
    kk = kk * lax.rsqrt(jnp.maximum(jnp.sum(kk * kk, axis=-1, keepdims=True), 1e-24))
    k = k * (1.0 + (a - 1.0) * k_a)
    rh, kh, vh, ah = hd(r), hd(k), hd(v), hd(a)
    y = wkv7_chunked(rh, hd(log_decay), kh, vh, -kk, kk * ah)
    mu = jnp.mean(y, axis=-1, keepdims=True)
    var = jnp.mean(jnp.square(y - mu), axis=-1, keepdims=True)
    y = ((y - mu) * lax.rsqrt(var + RW_LN_EPS)).reshape(B, T, RW_W) * ln_g + ln_b
    y = y + (jnp.sum(rh * kh * r_k, axis=-1, keepdims=True) * vh).reshape(B, T, RW_W)
    return (y * g).astype(h.dtype), v_first


def memory_cross_attention(h, mem_n, wq, wk, wv, wo):
    B, T, _ = h.shape
    M = mem_n.shape[1]
    q = (h @ wq).reshape(B, T, XA_HEADS, XA_DH)
    k = (mem_n @ wk).reshape(B, M, XA_HEADS, XA_DH)
    v = (mem_n @ wv).reshape(B, M, XA_HEADS, XA_DH)
    s = jnp.einsum('bthd,bmhd->bhtm', q, k).astype(jnp.float32) * (XA_DH ** -0.5)
    p = jax.nn.softmax(s, axis=-1)
    o = jnp.einsum('bhtm,bmhd->bthd', p.astype(v.dtype), v).reshape(B, T, XA_W)
    return o @ wo


def setup_inputs(seed: int = 0) -> dict:
    key = jax.random.key(seed)
    keys = iter(list(jax.random.split(key, 64)))
    f32 = jnp.float32
    L, D = DEPTH, D_MODEL

    def nrm(shape, scale):
        return jax.random.normal(next(keys), shape, f32) * scale

    def unif(shape, lo, hi):
        return jax.random.uniform(next(keys), shape, f32, lo, hi)

    def gain(shape):
        return 1.0 + nrm(shape, 0.02)

    return {
        'x': nrm((BATCH, SEQ, D), 1.0),
        'mem': nrm((BATCH, N_MEM, D), 1.0),
        'ffn1_pre': gain((L, D)),
        'ffn1_post': gain((L, D)),
        'ffn1_w1': nrm((L, D, D_FF), D ** -0.5),
        'ffn1_w3': nrm((L, D, D_FF), D ** -0.5),
        'ffn1_w2': nrm((L, D_FF, D), D_FF ** -0.5),
        'mix_pre': gain((L, D)),
        'mix_post': gain((L, D)),
        'w_in': nrm((L, D, D_IN), D ** -0.5),
        'gla_a1': nrm((L, D, GLA_GATE_RANK), D ** -0.5),
        'gla_a2': nrm((L, GLA_GATE_RANK, GLA_KW), GLA_GATE_RANK ** -0.5),
        'gla_ab': 1.0 + nrm((L, GLA_KW), 0.5),
        'gla_norm': gain((L, GLA_DV)),
        'rw_mu_rkv': unif((L, 3, RW_W), 0.0, 1.0),
        'rw_mu_wag': unif((L, 3, D), 0.0, 1.0),
        'rw_w0': unif((L, RW_W), -6.0, -1.0),
        'rw_w1': nrm((L, D, RW_DECAY_RANK), D ** -0.5),
        'rw_w2': nrm((L, RW_DECAY_RANK, RW_W), 0.1),
        'rw_a0': nrm((L, RW_W), 0.5),
        'rw_a1': nrm((L, D, RW_A_RANK), D ** -0.5),
        'rw_a2': nrm((L, RW_A_RANK, RW_W), 0.5 * RW_A_RANK ** -0.5),
        'rw_g1': nrm((L, D, RW_G_RANK), D ** -0.5),
        'rw_g2': nrm((L, RW_G_RANK, RW_W), RW_G_RANK ** -0.5),
        'rw_k_k': 0.85 + nrm((L, RW_W), 0.05),
        'rw_k_a': 1.0 + nrm((L, RW_W), 0.05),
        'rw_r_k': nrm((L, RW_HEADS, RW_DH), 0.1),
        'rw_ln_g': gain((L, RW_W)),
        'rw_ln_b': nrm((L, RW_W), 0.02),
        'rw_mu_vl': unif((L - 1, D), 0.0, 1.0),
        'rw_v0': nrm((L - 1, RW_W), 0.5),
        'rw_v1': nrm((L - 1, D, RW_V_RANK), D ** -0.5),
        'rw_v2': nrm((L - 1, RW_V_RANK, RW_W), 0.5 * RW_V_RANK ** -0.5),
        'branch_w': nrm((L, N_BRANCH, SB_W, D), SB_W ** -0.5),
        'gate_a': nrm((L, N_BRANCH, D, GATE_RANK), D ** -0.5),
        'gate_c': nrm((L, N_BRANCH, GATE_RANK, D), GATE_RANK ** -0.5),
        'gate_b': nrm((L, N_BRANCH, D), 0.02),
        'w_out': nrm((L, D, D), D ** -0.5),
        'xa_pre': gain((L, D)),
        'xa_post': gain((L, D)),
        'mem_norm': gain((L, D)),
        'xa_wq': nrm((L, D, XA_W), D ** -0.5),
        'xa_wk': nrm((L, D, XA_W), D ** -0.5),
        'xa_wv': nrm((L, D, XA_W), D ** -0.5),
        'xa_wo': nrm((L, XA_W, D), XA_W ** -0.5),
        'ffn2_pre': gain((L, D)),
        'ffn2_post': gain((L, D)),
        'ffn2_w1': nrm((L, D, D_FF), D ** -0.5),
        'ffn2_w3': nrm((L, D, D_FF), D ** -0.5),
        'ffn2_w2': nrm((L, D_FF, D), D_FF ** -0.5),
    }


def reference(x, mem, ffn1_pre, ffn1_post, ffn1_w1, ffn1_w3, ffn1_w2, mix_pre, mix_post, w_in,
              gla_a1, gla_a2, gla_ab, gla_norm, rw_mu_rkv, rw_mu_wag, rw_w0, rw_w1, rw_w2, rw_a0, rw_a1,
              rw_a2, rw_g1, rw_g2, rw_k_k, rw_k_a, rw_r_k, rw_ln_g, rw_ln_b, rw_mu_vl, rw_v0, rw_v1, rw_v2,
              branch_w, gate_a, gate_c, gate_b, w_out, xa_pre, xa_post, mem_norm, xa_wq, xa_wk, xa_wv, xa_wo,
              ffn2_pre, ffn2_post, ffn2_w1, ffn2_w3, ffn2_w2):
    v_first = None
    for l in range(DEPTH):
        h = rms_norm(x, ffn1_pre[l])
        x = x + 0.5 * rms_norm(swiglu(h, ffn1_w1[l], ffn1_w3[l], ffn1_w2[l]), ffn1_post[l])

        h = rms_norm(x, mix_pre[l])
        proj = h @ w_in[l]
        sb_q, sb_k, sb_v, gl_q, gl_k, gl_v, gl_g, rw_r, rw_k, rw_v = jnp.split(proj, SPLIT_POINTS, axis=-1)
        y_sb = stick_breaking_attention(sb_q, sb_k, sb_v)
        log_alpha = jax.nn.log_sigmoid(((h @ gla_a1[l]) @ gla_a2[l] + gla_ab[l]).astype(jnp.float32)) / GLA_TAU
        y_gla = gla_attention(gl_q, gl_k, gl_v, gl_g, log_alpha, gla_norm[l])
        vres = None if l == 0 else (rw_mu_vl[l - 1], rw_v0[l - 1], rw_v1[l - 1], rw_v2[l - 1])
        y_rw, v_first = rwkv7_time_mix(h, rw_r, rw_k, rw_v, v_first, vres, rw_mu_rkv[l], rw_mu_wag[l],
                                       rw_w0[l], rw_w1[l], rw_w2[l], rw_a0[l], rw_a1[l], rw_a2[l],
                                       rw_g1[l], rw_g2[l], rw_k_k[l], rw_k_a[l], rw_r_k[l],
                                       rw_ln_g[l], rw_ln_b[l])
        merged = (jax.nn.sigmoid((h @ gate_a[l, 0]) @ gate_c[l, 0] + gate_b[l, 0]) * (y_sb @ branch_w[l, 0])
                  + jax.nn.sigmoid((h @ gate_a[l, 1]) @ gate_c[l, 1] + gate_b[l, 1]) * (y_gla @ branch_w[l, 1])
                  + jax.nn.sigmoid((h @ gate_a[l, 2]) @ gate_c[l, 2] + gate_b[l, 2]) * (y_rw @ branch_w[l, 2]))
        x = x + rms_norm(merged @ w_out[l], mix_post[l])

        h = rms_norm(x, xa_pre[l])
        mem_n = rms_norm(mem, mem_norm[l])
        x = x + rms_norm(memory_cross_attention(h, mem_n, xa_wq[l], xa_wk[l], xa_wv[l], xa_wo[l]), xa_post[l])

        h = rms_norm(x, ffn2_pre[l])
        x = x + 0.5 * rms_norm(swiglu(h, ffn2_w1[l], ffn2_w3[l], ffn2_w2[l]), ffn2_post[l])
    return x
```

```python
import functools

import jax
import jax.numpy as jnp
from jax import lax
from jax.experimental import pallas as pl
from jax.experimental.pallas import tpu as pltpu

F32 = jnp.float32
BF16 = jnp.bfloat16

EPS = 1e-6
CHUNK = 64
SB_HEADS, SB_DH, SB_W = 4, 256, 1024
GLA_HEADS, GLA_DK, GLA_DV, GLA_KW, GLA_VW = 4, 128, 256, 512, 1024
GLA_TAU = 16.0
RW_HEADS, RW_DH, RW_W = 16, 64, 1024
RW_LN_EPS = 64e-5
RW_CHUNK = 16
XA_HEADS, XA_DH, XA_W = 4, 256, 1024
LANES = 128
VMEM_LIMIT = 56 * 1024 * 1024


def _params(*sem):
    return pltpu.CompilerParams(dimension_semantics=sem, vmem_limit_bytes=VMEM_LIMIT)


def _dot(a, b):
    return jnp.dot(a, b, preferred_element_type=F32)


def _dot_nt(a, b):
    return lax.dot_general(a, b, (((1,), (1,)), ((), ())), preferred_element_type=F32)


def _dot_tn(a, b):
    return lax.dot_general(a, b, (((0,), (0,)), ((), ())), preferred_element_type=F32)


def _split(x):
    hi = x.astype(BF16)
    lo = (x - hi.astype(F32)).astype(BF16)
    return hi, lo


def _softplus(z):
    return jnp.maximum(z, 0.0) + jnp.log1p(jnp.exp(-jnp.abs(z)))


def _sigmoid(z):
    return 1.0 / (1.0 + jnp.exp(-z))


def _tile(n, pref):
    t = min(n, pref)
    while n % t:
        t -= LANES
    return t


def _mm_kernel(a_ref, b_ref, o_ref):
    o_ref[...] = _dot(a_ref[...].astype(BF16), b_ref[...]).astype(o_ref.dtype)


def mm(a, b, out_dtype=F32, tm=1024, tn=512):
    M, K = a.shape
    N = b.shape[1]
    tm = min(tm, M)
    tn = _tile(N, tn)
    assert M % tm == 0 and N % tn == 0
    return pl.pallas_call(
        _mm_kernel,
        grid=(M // tm, N // tn),
        in_specs=[pl.BlockSpec((tm, K), lambda i, j: (i, 0)),
                  pl.BlockSpec((K, tn), lambda i, j: (0, j))],
        out_specs=pl.BlockSpec((tm, tn), lambda i, j: (i, j)),
        out_shape=jax.ShapeDtypeStruct((M, N), out_dtype),
        compiler_params=_params("parallel", "parallel"),
        name="mm",
    )(a, b)


def _rms(x, g):
    return x * lax.rsqrt(jnp.mean(x * x, axis=-1, keepdims=True) + EPS) * g


def _rmsnorm_kernel(x_ref, g_ref, o_ref):
    o_ref[...] = _rms(x_ref[...], g_ref[...]).astype(o_ref.dtype)


def rmsnorm(x, g, out_dtype=BF16, tm=256):
    M, D = x.shape
    tm = min(tm, M)
    return pl.pallas_call(
        _rmsnorm_kernel,
        grid=(M // tm,),
        in_specs=[pl.BlockSpec((tm, D), lambda i: (i, 0)),
                  pl.BlockSpec((1, D), lambda i: (0, 0))],
        out_specs=pl.BlockSpec((tm, D), lambda i: (i, 0)),
        out_shape=jax.ShapeDtypeStruct((M, D), out_dtype),
        compiler_params=_params("parallel"),
        name="rmsnorm",
    )(x, g.reshape(1, D))


def _resid_norm_kernel(x_ref, y_ref, gp_ref, gn_ref, xo_ref, h_ref, *, coef):
    xn = x_ref[...] + coef * _rms(y_ref[...], gp_ref[...])
    xo_ref[...] = xn
    h_ref[...] = _rms(xn, gn_ref[...]).astype(h_ref.dtype)


def resid_norm(x, y, g_post, g_next, coef, tm=256):
    M, D = x.shape
    tm = min(tm, M)
    row = pl.BlockSpec((tm, D), lambda i: (i, 0))
    vec = pl.BlockSpec((1, D), lambda i: (0, 0))
    return pl.pallas_call(
        functools.partial(_resid_norm_kernel, coef=coef),
        grid=(M // tm,),
        in_specs=[row, row, vec, vec],
        out_specs=[row, row],
        out_shape=[jax.ShapeDtypeStruct((M, D), F32), jax.ShapeDtypeStruct((M, D), BF16)],
        compiler_params=_params("parallel"),
        name="resid_norm",
    )(x, y, g_post.reshape(1, D), g_next.reshape(1, D))


def _ffn_kernel(h_ref, w1_ref, w3_ref, w2_ref, o_ref):
    h = h_ref[...]
    a = _dot(h, w1_ref[...])
    b = _dot(h, w3_ref[...])
    u = (a * _sigmoid(a) * b).astype(BF16)
    part = _dot(u, w2_ref[...])

    @pl.when(pl.program_id(1) == 0)
    def _():
        o_ref[...] = part

    @pl.when(pl.program_id(1) != 0)
    def _():
        o_ref[...] += part


def ffn(h, w1, w3, w2, tm=512, tf=512):
    M, D = h.shape
    Fd = w1.shape[1]
    tm = min(tm, M)
    tf = min(tf, Fd)
    return pl.pallas_call(
        _ffn_kernel,
        grid=(M // tm, Fd // tf),
        in_specs=[pl.BlockSpec((tm, D), lambda i, j: (i, 0)),
                  pl.BlockSpec((D, tf), lambda i, j: (0, j)),
                  pl.BlockSpec((D, tf), lambda i, j: (0, j)),
                  pl.BlockSpec((tf, D), lambda i, j: (j, 0))],
        out_specs=pl.BlockSpec((tm, D), lambda i, j: (i, 0)),
        out_shape=jax.ShapeDtypeStruct((M, D), F32),
        compiler_params=_params("parallel", "arbitrary"),
        name="ffn",
    )(h, w1, w3, w2)


def _merge_kernel(ha_ref, ysb_ref, ygl_ref, yrw_ref, gc_ref, gb_ref, bw_ref, wo_ref, o_ref, *, rank):
    ha = ha_ref[...]
    m = None
    for g, y_ref in enumerate((ysb_ref, ygl_ref, yrw_ref)):
        gate = _sigmoid(_dot(ha[:, g * rank:(g + 1) * rank], gc_ref[g]) + gb_ref[g])
        term = gate * _dot(y_ref[...], bw_ref[g])
        m = term if m is None else m + term
    part = _dot(m.astype(BF16), wo_ref[...])

    @pl.when(pl.program_id(1) == 0)
    def _():
        o_ref[...] = part

    @pl.when(pl.program_id(1) != 0)
    def _():
        o_ref[...] += part


def merge(ha, y_sb, y_gla, y_rw, gate_c, gate_b, branch_w, w_out, tm=512, tn=512):
    M = ha.shape[0]
    nb, rank, D = gate_c.shape
    W = branch_w.shape[1]
    tm = min(tm, M)
    tn = min(tn, D)
    ybs = pl.BlockSpec((tm, W), lambda i, j: (i, 0))
    return pl.pallas_call(
        functools.partial(_merge_kernel, rank=rank),
        grid=(M // tm, D // tn),
        in_specs=[pl.BlockSpec((tm, nb * rank), lambda i, j: (i, 0)), ybs, ybs, ybs,
                  pl.BlockSpec((nb, rank, tn), lambda i, j: (0, 0, j)),
                  pl.BlockSpec((nb, 1, tn), lambda i, j: (0, 0, j)),
                  pl.BlockSpec((nb, W, tn), lambda i, j: (0, 0, j)),
                  pl.BlockSpec((tn, D), lambda i, j: (j, 0))],
        out_specs=pl.BlockSpec((tm, D), lambda i, j: (i, 0)),
        out_shape=jax.ShapeDtypeStruct((M, D), F32),
        compiler_params=_params("parallel", "arbitrary"),
        name="merge",
    )(ha, y_sb, y_gla, y_rw, gate_c, gate_b.reshape(nb, 1, D), branch_w, w_out)


def _xattn_kernel(h_ref, wq_ref, k_ref, v_ref, wo_ref, o_ref):
    q = _dot(h_ref[...], wq_ref[...])
    outs = []
    for hh in range(XA_HEADS):
        sl = slice(hh * XA_DH, (hh + 1) * XA_DH)
        s = _dot_nt(q[:, sl].astype(BF16), k_ref[:, sl]) * (XA_DH ** -0.5)
        e = jnp.exp(s - jnp.max(s, axis=-1, keepdims=True))
        p = e / jnp.sum(e, axis=-1, keepdims=True)
        outs.append(_dot(p.astype(BF16), v_ref[:, sl]).astype(BF16))
    o_ref[...] = _dot(jnp.concatenate(outs, axis=1), wo_ref[...])


def xattn(h, wq, k, v, wo, tm=256):
    M, D = h.shape
    nm = k.shape[0]
    tm = min(tm, M)
    return pl.pallas_call(
        _xattn_kernel,
        grid=(M // tm,),
        in_specs=[pl.BlockSpec((tm, D), lambda i: (i, 0)),
                  pl.BlockSpec((D, XA_W), lambda i: (0, 0)),
                  pl.BlockSpec((nm, XA_W), lambda i: (0, 0)),
                  pl.BlockSpec((nm, XA_W), lambda i: (0, 0)),
                  pl.BlockSpec((XA_W, D), lambda i: (0, 0))],
        out_specs=pl.BlockSpec((tm, D), lambda i: (i, 0)),
        out_shape=jax.ShapeDtypeStruct((M, D), F32),
        compiler_params=_params("parallel"),
        name="xattn",
    )(h, wq, k, v, wo)


def _sb_kernel(qi_tab, kj_tab, q_ref, k_ref, v_ref, o_ref, acc_ref, carry_ref, *, tb):
    step = pl.program_id(1)
    qi = qi_tab[step]
    kj = kj_tab[step]
    sub = LANES
    nsub = tb // sub

    @pl.when(kj == qi)
    def _():
        acc_ref[...] = jnp.zeros_like(acc_ref)
        carry_ref[...] = jnp.zeros_like(carry_ref)

    def body(diag):
        q = q_ref[...]
        jj = lax.broadcasted_iota(jnp.int32, (sub, sub), 0)
        ss = lax.broadcasted_iota(jnp.int32, (sub, sub), 1)
        tri = jnp.where(jj >= ss, 1.0, 0.0).astype(BF16)
        row = lax.broadcasted_iota(jnp.int32, (tb, sub), 0)
        col = lax.broadcasted_iota(jnp.int32, (tb, sub), 1)
        carry = carry_ref[...]
        acc = acc_ref[...]
        for c in reversed(range(nsub)):
            kc = k_ref[c * sub:(c + 1) * sub, :]
            vc = v_ref[c * sub:(c + 1) * sub, :]
            z = _dot_nt(q, kc) * (SB_DH ** -0.5)
            sp = _softplus(z)
            if diag:
                causal = (col + c * sub) < row
                lk = jnp.where(causal, -sp, 0.0)
            else:
                lk = -sp
            hi, lo = _split(lk)
            suffix = _dot(hi, tri) + _dot(lo, tri)
            later = suffix - lk + carry
            a = jnp.exp(z - sp + later)
            if diag:
                a = jnp.where(causal, a, 0.0)
            acc = acc + _dot(a.astype(BF16), vc)
            carry = carry + suffix[:, 0:1]
        acc_ref[...] = acc
        carry_ref[...] = carry

    @pl.when(kj == qi)
    def _():
        body(True)

    @pl.when(kj != qi)
    def _():
        body(False)

    @pl.when(kj == 0)
    def _():
        o_ref[...] = acc_ref[...].astype(o_ref.dtype)


def sb_attention(p_sb, tb=512):
    T = p_sb.shape[0]
    tb = min(tb, T)
    nq = T // tb
    qi_l, kj_l = [], []
    for i in range(nq):
        for j in range(i, -1, -1):
            qi_l.append(i)
            kj_l.append(j)
    qi_tab = jnp.asarray(qi_l, jnp.int32)
    kj_tab = jnp.asarray(kj_l, jnp.int32)
    H = SB_HEADS
    grid_spec = pltpu.PrefetchScalarGridSpec(
        num_scalar_prefetch=2,
        grid=(H, len(qi_l)),
        in_specs=[pl.BlockSpec((tb, SB_DH), lambda h, s, qt, kt: (qt[s], h)),
                  pl.BlockSpec((tb, SB_DH), lambda h, s, qt, kt: (kt[s], H + h)),
                  pl.BlockSpec((tb, SB_DH), lambda h, s, qt, kt: (kt[s], 2 * H + h))],
        out_specs=pl.BlockSpec((tb, SB_DH), lambda h, s, qt, kt: (qt[s], h)),
        scratch_shapes=[pltpu.VMEM((tb, SB_DH), F32), pltpu.VMEM((tb, 1), F32)],
    )
    return pl.pallas_call(
        functools.partial(_sb_kernel, tb=tb),
        grid_spec=grid_spec,
        out_shape=jax.ShapeDtypeStruct((T, SB_W), BF16),
        compiler_params=_params("parallel", "arbitrary"),
        name="sb_attention",
    )(qi_tab, kj_tab, p_sb, p_sb, p_sb)


def _gla_kernel(q_ref, k_ref, v_ref, g_ref, ap_ref, a2_ref, ab_ref, ng_ref, o_ref, s_ref, *, tb):
    C = CHUNK

    @pl.when(pl.program_id(1) == 0)
    def _():
        s_ref[...] = jnp.zeros_like(s_ref)

    tt = lax.broadcasted_iota(jnp.int32, (C, C), 0)
    ss = lax.broadcasted_iota(jnp.int32, (C, C), 1)
    causal = ss <= tt
    tri = jnp.where(causal, 1.0, 0.0).astype(BF16)
    st = s_ref[...]
    for c in range(tb // C):
        sl = slice(c * C, (c + 1) * C)
        pre = _dot(ap_ref[sl, :].astype(BF16), a2_ref[...]) + ab_ref[...]
        la = -_softplus(-pre) * (1.0 / GLA_TAU)
        hi, lo = _split(la)
        b = _dot(tri, hi) + _dot(tri, lo)
        b_last = b[C - 1:C, :]
        k = k_ref[sl, :]
        q_in = (q_ref[sl, :] * (GLA_DK ** -0.5) * jnp.exp(b)).astype(BF16)
        k_in = (k * jnp.exp(-b)).astype(BF16)
        k_end = (k * jnp.exp(b_last - b)).astype(BF16)
        v = v_ref[sl, :].astype(BF16)
        scores = jnp.where(causal, _dot_nt(q_in, k_in), 0.0)
        o = _dot(scores.astype(BF16), v) + _dot_nt(q_in, st.astype(BF16))
        st = jnp.exp(b_last) * st + _dot_tn(v, k_end)
        o = o * lax.rsqrt(jnp.mean(o * o, axis=-1, keepdims=True) + EPS) * ng_ref[...]
        g = g_ref[sl, :]
        o_ref[sl, :] = (o * (g * _sigmoid(g))).astype(o_ref.dtype)
    s_ref[...] = st


def gla(p_gla, a2p, ab, norm_g, tb=512):
    T = p_gla.shape[0]
    tb = min(tb, T)
    H = GLA_HEADS
    kq = GLA_KW // GLA_DK
    kv = 2 * GLA_KW // GLA_DV
    ap_blk = (2 * GLA_KW + 2 * GLA_VW) // LANES
    return pl.pallas_call(
        functools.partial(_gla_kernel, tb=tb),
        grid=(H, T // tb),
        in_specs=[pl.BlockSpec((tb, GLA_DK), lambda h, i: (i, h)),
                  pl.BlockSpec((tb, GLA_DK), lambda h, i: (i, kq + h)),
                  pl.BlockSpec((tb, GLA_DV), lambda h, i: (i, kv + h)),
                  pl.BlockSpec((tb, GLA_DV), lambda h, i: (i, kv + H + h)),
                  pl.BlockSpec((tb, LANES), lambda h, i: (i, ap_blk)),
                  pl.BlockSpec((LANES, GLA_DK), lambda h, i: (0, h)),
                  pl.BlockSpec((1, GLA_DK), lambda h, i: (0, h)),
                  pl.BlockSpec((1, GLA_DV), lambda h, i: (0, 0))],
        out_specs=pl.BlockSpec((tb, GLA_DV), lambda h, i: (i, h)),
        out_shape=jax.ShapeDtypeStruct((T, GLA_VW), BF16),
        scratch_shapes=[pltpu.VMEM((GLA_DV, GLA_DK), F32)],
        compiler_params=_params("parallel", "arbitrary"),
        name="gla",
    )(p_gla, p_gla, p_gla, p_gla, p_gla, a2p, ab.reshape(1, GLA_KW), norm_g.reshape(1, GLA_DV))


def _wkv_kernel(r_ref, lw_ref, k_ref, v_ref, a_ref, b_ref, o_ref, s_ref, *, R):
    C = RW_CHUNK
    N = RW_DH
    nch = R // C

    @pl.when(pl.program_id(1) == 0)
    def _():
        s_ref[...] = jnp.zeros_like(s_ref)

    r, lw, k, v, a, b = (ref[0] for ref in (r_ref, lw_ref, k_ref, v_ref, a_ref, b_ref))
    row = lax.broadcasted_iota(jnp.int32, (R, R), 0)
    col = lax.broadcasted_iota(jnp.int32, (R, R), 1)
    same = (row // C) == (col // C)
    m_incl = same & (col <= row)
    m_strict = same & (col < row)
    m_incl_t = same & (row <= col)
    one = lambda m: jnp.where(m, 1.0, 0.0).astype(BF16)

    lw_hi, lw_lo = _split(lw)
    g_in = _dot(one(m_incl), lw_hi) + _dot(one(m_incl), lw_lo)
    g_last = _dot(one(same), lw_hi) + _dot(one(same), lw_lo)
    r_d = (r * jnp.exp(g_in)).astype(BF16)
    a_d = (a * jnp.exp(g_in - lw)).astype(BF16)
    inv = jnp.exp(-g_in)
    b_h = (b * inv).astype(BF16)
    k_h = (k * inv).astype(BF16)
    to_end = jnp.exp(g_last - g_in)
    b_e = (b * to_end).astype(BF16)
    k_e = (k * to_end).astype(BF16)
    dec = jnp.exp(g_last)

    gram = _dot_nt(jnp.concatenate([a_d, r_d], axis=0), jnp.concatenate([b_h, k_h], axis=0))
    l_ab = jnp.where(m_strict, gram[:R, :R], 0.0)
    l_ak = jnp.where(m_strict, gram[:R, R:], 0.0).astype(BF16)
    m_rk = jnp.where(m_incl, gram[R:, R:], 0.0).astype(BF16)
    m_rb_t = jnp.where(m_incl_t, _dot_nt(b_h, r_d), 0.0).astype(BF16)

    inv_m = jnp.where(row == col, 1.0, 0.0) + l_ab
    p = l_ab
    steps = C.bit_length() - 2
    for _ in range(steps):
        pb = p.astype(BF16)
        p = _dot(pb, pb)
        inv_m = inv_m + _dot(inv_m.astype(BF16), p.astype(BF16))
    inv_b = inv_m.astype(BF16)

    ri = lax.broadcasted_iota(jnp.int32, (N, N), 0)
    ci = lax.broadcasted_iota(jnp.int32, (N, N), 1)
    eye_n = jnp.where(ri == ci, 1.0, 0.0).astype(BF16)
    xt = _dot_nt(eye_n, jnp.concatenate([v.astype(BF16), a_d, r_d], axis=0))
    v_t = xt[:, :R].astype(BF16)
    a_dt = xt[:, R:2 * R].astype(BF16)
    r_dt = xt[:, 2 * R:].astype(BF16)

    lv_t = _dot_nt(v_t, l_ak).astype(BF16)
    w_at = _dot_nt(a_dt, inv_b).astype(BF16)
    u_vt = _dot_nt(lv_t, inv_b)
    o_loct = _dot_nt(v_t, m_rk)

    lane_chunk = lax.broadcasted_iota(jnp.int32, (N, R), 1) // C
    v_stack = jnp.concatenate(
        [jnp.where(lane_chunk == c, v_t, jnp.zeros_like(v_t)) for c in range(nch)], axis=0)
    kv = _dot(v_stack, k_e)

    rhs = jnp.concatenate([w_at, r_dt], axis=1)
    st = s_ref[...]
    u_all = jnp.zeros((N, R), F32)
    o_s = jnp.zeros((N, R), F32)
    for c in range(nch):
        res = _dot(st.astype(BF16), rhs)
        m = lane_chunk == c
        u_c = jnp.where(m, res[:, :R] + u_vt, 0.0)
        u_all = u_all + u_c
        o_s = jnp.where(m, res[:, R:], o_s)
        st = dec[c * C:c * C + 1, :] * st + _dot(u_c.astype(BF16), b_e) + kv[c * N:(c + 1) * N, :]
    s_ref[...] = st

    o_t = o_s + _dot(u_all.astype(BF16), m_rb_t) + o_loct
    rr = lax.broadcasted_iota(jnp.int32, (R, R), 0)
    cc = lax.broadcasted_iota(jnp.int32, (R, R), 1)
    eye_r = jnp.where(rr == cc, 1.0, 0.0).astype(BF16)
    hi, lo = _split(o_t)
    o_ref[0] = _dot_nt(eye_r, hi) + _dot_nt(eye_r, lo)


def wkv7(r, lw, k, v, a, b, R=128):
    H, T, N = r.shape
    R = min(R, T)
    spec = pl.BlockSpec((1, R, N), lambda h, i: (h, i, 0))
    return pl.pallas_call(
        functools.partial(_wkv_kernel, R=R),
        grid=(H, T // R),
        in_specs=[spec] * 6,
        out_specs=spec,
        out_shape=jax.ShapeDtypeStruct((H, T, N), F32),
        scratch_shapes=[pltpu.VMEM((N, N), F32)],
        compiler_params=_params("parallel", "arbitrary"),
        name="wkv7",
    )(r, lw, k, v, a, b)


def _shift(x):
    return jnp.pad(x, ((1, 0), (0, 0)))[:-1]


def _heads(x):
    T = x.shape[0]
    return x.reshape(T, RW_HEADS, RW_DH).transpose(1, 0, 2)


def rwkv_time_mix(p_rw, ranks, v_first, w):
    T = p_rw.shape[0]
    r_p, k_p, v_p = (p_rw[:, i * RW_W:(i + 1) * RW_W] for i in range(3))
    nlr = sum(ranks)
    low = p_rw[:, 3 * RW_W:3 * RW_W + nlr] + _shift(p_rw[:, 3 * RW_W + nlr:3 * RW_W + 2 * nlr])
    offs = [0]
    for rk in ranks:
        offs.append(offs[-1] + rk)
    low_w, low_a, low_g = (low[:, offs[i]:offs[i + 1]] for i in range(3))
    mu = w['mu_rkv']
    r = r_p + (_shift(r_p) - r_p) * mu[0]
    k = k_p + (_shift(k_p) - k_p) * mu[1]
    v = v_p + (_shift(v_p) - v_p) * mu[2]
    w_log = -jax.nn.softplus(-(w['w0'] + mm(jnp.tanh(low_w), w['w2']))) - 0.5
    log_decay = -jnp.exp(w_log)
    if v_first is None:
        v_first = v
    else:
        low_v = low[:, offs[3]:offs[4]]
        v = v + (v_first - v) * jax.nn.sigmoid(w['v0'] + mm(low_v, w['v2']))
    a = jax.nn.sigmoid(w['a0'] + mm(low_a, w['a2']))
    g = mm(jax.nn.sigmoid(low_g), w['g2'])

    kk = (k * w['k_k']).reshape(T, RW_HEADS, RW_DH)
    kk = (kk * lax.rsqrt(jnp.maximum(jnp.sum(kk * kk, axis=-1, keepdims=True), 1e-24))).reshape(T, RW_W)
    k = k * (1.0 + (a - 1.0) * w['k_a'])
    y = wkv7(_heads(r), _heads(log_decay), _heads(k), _heads(v), _heads(-kk), _heads(kk * a))
    y = y.transpose(1, 0, 2)
    mean = jnp.mean(y, axis=-1, keepdims=True)
    var = jnp.mean(jnp.square(y - mean), axis=-1, keepdims=True)
    y = ((y - mean) * lax.rsqrt(var + RW_LN_EPS)).reshape(T, RW_W) * w['ln_g'] + w['ln_b']
    rk = (r * k).reshape(T, RW_HEADS, RW_DH) * w['r_k']
    bonus = jnp.sum(rk, axis=-1, keepdims=True) * v.reshape(T, RW_HEADS, RW_DH)
    y = y + bonus.reshape(T, RW_W)
    return (y * g).astype(BF16), v_first


def _pad_cols(w, mult):
    n = w.shape[1]
    pad = (-n) % mult
    return w if pad == 0 else jnp.pad(w, ((0, 0), (0, pad)))


def kernel(x, mem, ffn1_pre, ffn1_post, ffn1_w1, ffn1_w3, ffn1_w2, mix_pre, mix_post, w_in, gla_a1, gla_a2, gla_ab, gla_norm, rw_mu_rkv, rw_mu_wag, rw_w0, rw_w1, rw_w2, rw_a0, rw_a1, rw_a2, rw_g1, rw_g2, rw_k_k, rw_k_a, rw_r_k, rw_ln_g, rw_ln_b, rw_mu_vl, rw_v0, rw_v1, rw_v2, branch_w, gate_a, gate_c, gate_b, w_out, xa_pre, xa_post, mem_norm, xa_wq, xa_wk, xa_wv, xa_wo, ffn2_pre, ffn2_post, ffn2_w1, ffn2_w3, ffn2_w2):
    B, T, D = x.shape
    depth = ffn1_pre.shape[0]
    bf = lambda t: t.astype(BF16)
    sb_end = 3 * SB_W
    gla_end = sb_end + 2 * GLA_KW + 2 * GLA_VW
    outs = []
    for bi in range(B):
        xs = x[bi]
        mem_b = mem[bi]
        h = rmsnorm(xs, ffn1_pre[0])
        v_first = None
        for l in range(depth):
            y = ffn(h, bf(ffn1_w1[l]), bf(ffn1_w3[l]), bf(ffn1_w2[l]))
            xs, h = resid_norm(xs, y, ffn1_post[l], mix_pre[l], 0.5)

            p_sb = mm(h, bf(w_in[l][:, :sb_end]), BF16)
            w_gla = jnp.concatenate([w_in[l][:, sb_end:gla_end], _pad_cols(gla_a1[l], LANES)], axis=1)
            p_gla = mm(h, bf(w_gla))
            lows = [(rw_w1[l], rw_mu_wag[l, 0]), (rw_a1[l], rw_mu_wag[l, 1]), (rw_g1[l], rw_mu_wag[l, 2])]
            if l > 0:
                lows.append((rw_v1[l - 1], rw_mu_vl[l - 1]))
            ranks = [wl.shape[1] for wl, _ in lows]
            w_rw = jnp.concatenate([w_in[l][:, gla_end:]]
                                   + [wl * (1.0 - m)[:, None] for wl, m in lows]
                                   + [wl * m[:, None] for wl, m in lows], axis=1)
            p_rw = mm(h, bf(_pad_cols(w_rw, 512)))
            ha = mm(h, bf(jnp.concatenate([gate_a[l, g] for g in range(gate_a.shape[1])], axis=1)), BF16)

            y_sb = sb_attention(p_sb)
            a2p = jnp.pad(gla_a2[l], ((0, LANES - gla_a2.shape[1]), (0, 0)))
            y_gla = gla(p_gla, bf(a2p), gla_ab[l], gla_norm[l])
            rw_w = dict(mu_rkv=rw_mu_rkv[l], w0=rw_w0[l], w2=bf(rw_w2[l]), a0=rw_a0[l], a2=bf(rw_a2[l]),
                        g2=bf(rw_g2[l]), k_k=rw_k_k[l], k_a=rw_k_a[l], r_k=rw_r_k[l],
                        ln_g=rw_ln_g[l], ln_b=rw_ln_b[l])
            if l > 0:
                rw_w.update(v0=rw_v0[l - 1], v2=bf(rw_v2[l - 1]))
            y_rw, v_first = rwkv_time_mix(p_rw, ranks, v_first, rw_w)

            y = merge(ha, y_sb, y_gla, y_rw, bf(gate_c[l]), gate_b[l], bf(branch_w[l]), bf(w_out[l]))
            xs, h = resid_norm(xs, y, mix_post[l], xa_pre[l], 1.0)

            mem_n = rmsnorm(mem_b, mem_norm[l])
            kx = mm(mem_n, bf(xa_wk[l]), BF16)
            vx = mm(mem_n, bf(xa_wv[l]), BF16)
            y = xattn(h, bf(xa_wq[l]), kx, vx, bf(xa_wo[l]))
            xs, h = resid_norm(xs, y, xa_post[l], ffn2_pre[l], 1.0)

            y = ffn(h, bf(ffn2_w1[l]), bf(ffn2_w3[l]), bf(ffn2_w2[l]))
            g_next = ffn1_pre[l + 1] if l + 1 < depth else ffn1_pre[0]
            xs, h = resid_norm(xs, y, ffn2_post[l], g_next, 0.5)
        outs.append(xs)
    return jnp.stack(outs, axis=0)
```

```python
import functools
import math

import jax
import jax.numpy as jnp
from jax import lax
from jax.experimental import pallas as pl
from jax.experimental.pallas import tpu as pltpu

F32 = jnp.float32
BF16 = jnp.bfloat16

EPS = 1e-6
CHUNK = 64
SB_HEADS, SB_DH, SB_W = 4, 256, 1024
GLA_HEADS, GLA_DK, GLA_DV, GLA_KW, GLA_VW = 4, 128, 256, 512, 1024
GLA_TAU = 16.0
RW_HEADS, RW_DH, RW_W = 16, 64, 1024
RW_LN_EPS = 64e-5
RW_CHUNK = 16
XA_HEADS, XA_DH, XA_W = 4, 256, 1024
LANES = 128
VMEM_LIMIT = 56 * 1024 * 1024


def _params(*sem):
    return pltpu.CompilerParams(dimension_semantics=sem, vmem_limit_bytes=VMEM_LIMIT)


def _dot(a, b):
    return jnp.dot(a, b, preferred_element_type=F32)


def _dot_nt(a, b):
    return lax.dot_general(a, b, (((1,), (1,)), ((), ())), preferred_element_type=F32)


def _dot_tn(a, b):
    return lax.dot_general(a, b, (((0,), (0,)), ((), ())), preferred_element_type=F32)


def _split(x):
    hi = x.astype(BF16)
    lo = (x - hi.astype(F32)).astype(BF16)
    return hi, lo


def _softplus(z):
    return jnp.maximum(z, 0.0) + jnp.log1p(jnp.exp(-jnp.abs(z)))


def _sigmoid(z):
    return 1.0 / (1.0 + jnp.exp(-z))


def _tile(n, pref):
    t = min(n, pref)
    while n % t:
        t -= LANES
    return t


def _mm_kernel(a_ref, b_ref, o_ref):
    o_ref[...] = _dot(a_ref[...].astype(BF16), b_ref[...]).astype(o_ref.dtype)


def mm(a, b, out_dtype=F32, tm=1024, tn=512):
    M, K = a.shape
    N = b.shape[1]
    tm = min(tm, M)
    tn = _tile(N, tn)
    assert M % tm == 0 and N % tn == 0
    return pl.pallas_call(
        _mm_kernel,
        grid=(M // tm, N // tn),
        in_specs=[pl.BlockSpec((tm, K), lambda i, j: (i, 0)),
                  pl.BlockSpec((K, tn), lambda i, j: (0, j))],
        out_specs=pl.BlockSpec((tm, tn), lambda i, j: (i, j)),
        out_shape=jax.ShapeDtypeStruct((M, N), out_dtype),
        compiler_params=_params("parallel", "parallel"),
        name="mm",
    )(a, b)


def _rms(x, g):
    return x * lax.rsqrt(jnp.mean(x * x, axis=-1, keepdims=True) + EPS) * g


def _rmsnorm_kernel(x_ref, g_ref, o_ref):
    o_ref[...] = _rms(x_ref[...], g_ref[...]).astype(o_ref.dtype)


def rmsnorm(x, g, out_dtype=BF16, tm=256):
    M, D = x.shape
    tm = min(tm, M)
    return pl.pallas_call(
        _rmsnorm_kernel,
        grid=(M // tm,),
        in_specs=[pl.BlockSpec((tm, D), lambda i: (i, 0)),
                  pl.BlockSpec((1, D), lambda i: (0, 0))],
        out_specs=pl.BlockSpec((tm, D), lambda i: (i, 0)),
        out_shape=jax.ShapeDtypeStruct((M, D), out_dtype),
        compiler_params=_params("parallel"),
        name="rmsnorm",
    )(x, g.reshape(1, D))


def _resid_norm_kernel(x_ref, y_ref, gp_ref, gn_ref, xo_ref, h_ref, *, coef):
    xn = x_ref[...] + coef * _rms(y_ref[...], gp_ref[...])
    xo_ref[...] = xn
    h_ref[...] = _rms(xn, gn_ref[...]).astype(h_ref.dtype)


def resid_norm(x, y, g_post, g_next, coef, tm=256):
    M, D = x.shape
    tm = min(tm, M)
    row = pl.BlockSpec((tm, D), lambda i: (i, 0))
    vec = pl.BlockSpec((1, D), lambda i: (0, 0))
    return pl.pallas_call(
        functools.partial(_resid_norm_kernel, coef=coef),
        grid=(M // tm,),
        in_specs=[row, row, vec, vec],
        out_specs=[row, row],
        out_shape=[jax.ShapeDtypeStruct((M, D), F32), jax.ShapeDtypeStruct((M, D), BF16)],
        compiler_params=_params("parallel"),
        name="resid_norm",
    )(x, y, g_post.reshape(1, D), g_next.reshape(1, D))


def _ffn_kernel(h_ref, w1_ref, w3_ref, w2_ref, o_ref):
    h = h_ref[...]
    a = _dot(h, w1_ref[...])
    b = _dot(h, w3_ref[...])
    u = (a * _sigmoid(a) * b).astype(BF16)
    part = _dot(u, w2_ref[...])

    @pl.when(pl.program_id(1) == 0)
    def _():
        o_ref[...] = part

    @pl.when(pl.program_id(1) != 0)
    def _():
        o_ref[...] += part


def ffn(h, w1, w3, w2, tm=512, tf=512):
    M, D = h.shape
    Fd = w1.shape[1]
    tm = min(tm, M)
    tf = min(tf, Fd)
    return pl.pallas_call(
        _ffn_kernel,
        grid=(M // tm, Fd // tf),
        in_specs=[pl.BlockSpec((tm, D), lambda i, j: (i, 0)),
                  pl.BlockSpec((D, tf), lambda i, j: (0, j)),
                  pl.BlockSpec((D, tf), lambda i, j: (0, j)),
                  pl.BlockSpec((tf, D), lambda i, j: (j, 0))],
        out_specs=pl.BlockSpec((tm, D), lambda i, j: (i, 0)),
        out_shape=jax.ShapeDtypeStruct((M, D), F32),
        compiler_params=_params("parallel", "arbitrary"),
        name="ffn",
    )(h, w1, w3, w2)


def _merge_kernel(ha_ref, ysb_ref, ygl_ref, yrw_ref, gc_ref, gb_ref, bw_ref, wo_ref, o_ref, *, rank):
    ha = ha_ref[...]
    m = None
    for g, y_ref in enumerate((ysb_ref, ygl_ref, yrw_ref)):
        gate = _sigmoid(_dot(ha[:, g * rank:(g + 1) * rank], gc_ref[g]) + gb_ref[g])
        term = gate * _dot(y_ref[...], bw_ref[g])
        m = term if m is None else m + term
    part = _dot(m.astype(BF16), wo_ref[...])

    @pl.when(pl.program_id(1) == 0)
    def _():
        o_ref[...] = part

    @pl.when(pl.program_id(1) != 0)
    def _():
        o_ref[...] += part


def merge(ha, y_sb, y_gla, y_rw, gate_c, gate_b, branch_w, w_out, tm=512, tn=512):
    M = ha.shape[0]
    nb, rank, D = gate_c.shape
    W = branch_w.shape[1]
    tm = min(tm, M)
    tn = min(tn, D)
    ybs = pl.BlockSpec((tm, W), lambda i, j: (i, 0))
    return pl.pallas_call(
        functools.partial(_merge_kernel, rank=rank),
        grid=(M // tm, D // tn),
        in_specs=[pl.BlockSpec((tm, nb * rank), lambda i, j: (i, 0)), ybs, ybs, ybs,
                  pl.BlockSpec((nb, rank, tn), lambda i, j: (0, 0, j)),
                  pl.BlockSpec((nb, 1, tn), lambda i, j: (0, 0, j)),
                  pl.BlockSpec((nb, W, tn), lambda i, j: (0, 0, j)),
                  pl.BlockSpec((tn, D), lambda i, j: (j, 0))],
        out_specs=pl.BlockSpec((tm, D), lambda i, j: (i, 0)),
        out_shape=jax.ShapeDtypeStruct((M, D), F32),
        compiler_params=_params("parallel", "arbitrary"),
        name="merge",
    )(ha, y_sb, y_gla, y_rw, gate_c, gate_b.reshape(nb, 1, D), branch_w, w_out)


def _xattn_kernel(h_ref, wq_ref, k_ref, v_ref, wo_ref, o_ref):
    q = _dot(h_ref[...], wq_ref[...])
    outs = []
    for hh in range(XA_HEADS):
        sl = slice(hh * XA_DH, (hh + 1) * XA_DH)
        s = _dot_nt(q[:, sl].astype(BF16), k_ref[:, sl]) * (XA_DH ** -0.5)
        e = jnp.exp(s - jnp.max(s, axis=-1, keepdims=True))
        p = e / jnp.sum(e, axis=-1, keepdims=True)
        outs.append(_dot(p.astype(BF16), v_ref[:, sl]).astype(BF16))
    o_ref[...] = _dot(jnp.concatenate(outs, axis=1), wo_ref[...])


def xattn(h, wq, k, v, wo, tm=256):
    M, D = h.shape
    nm = k.shape[0]
    tm = min(tm, M)
    return pl.pallas_call(
        _xattn_kernel,
        grid=(M // tm,),
        in_specs=[pl.BlockSpec((tm, D), lambda i: (i, 0)),
                  pl.BlockSpec((D, XA_W), lambda i: (0, 0)),
                  pl.BlockSpec((nm, XA_W), lambda i: (0, 0)),
                  pl.BlockSpec((nm, XA_W), lambda i: (0, 0)),
                  pl.BlockSpec((XA_W, D), lambda i: (0, 0))],
        out_specs=pl.BlockSpec((tm, D), lambda i: (i, 0)),
        out_shape=jax.ShapeDtypeStruct((M, D), F32),
        compiler_params=_params("parallel"),
        name="xattn",
    )(h, wq, k, v, wo)


SB_SCALE = SB_DH ** -0.5
assert math.frexp(SB_SCALE)[0] == 0.5, "q pre-scaling is only exact for a power-of-two scale"
SB_SUB = 256


def _sb_kernel(qi_tab, kj_tab, q_ref, k_ref, v_ref, o_ref, acc_ref, carry_ref, a_ref, *, tb):
    step = pl.program_id(1)
    qi = qi_tab[step]
    kj = kj_tab[step]
    sub = min(SB_SUB, tb)
    nsub = tb // sub

    @pl.when(kj == qi)
    def _():
        acc_ref[...] = jnp.zeros_like(acc_ref)
        carry_ref[...] = jnp.zeros_like(carry_ref)

    def body(diag):
        z_full = _dot_nt(q_ref[...], k_ref[...])
        jj = lax.broadcasted_iota(jnp.int32, (sub, sub), 0)
        ss = lax.broadcasted_iota(jnp.int32, (sub, sub), 1)
        neg_tri = jnp.where(jj > ss, -1.0, 0.0).astype(BF16)
        for c in reversed(range(nsub)):
            r0 = c * sub if diag else 0
            cs = slice(c * sub, (c + 1) * sub)
            z = z_full[r0:, cs]
            sp = jnp.maximum(z, 0.0) + jnp.log(1.0 + jnp.exp(-jnp.abs(z)))
            if diag:
                row = lax.broadcasted_iota(jnp.int32, z.shape, 0)
                col = lax.broadcasted_iota(jnp.int32, z.shape, 1)
                causal = col < row + (r0 - c * sub)
                sp = jnp.where(causal, sp, 0.0)
            later = _dot(sp.astype(BF16), neg_tri) + carry_ref[r0:, :]
            a = jnp.exp(z - sp + later)
            if diag:
                a = jnp.where(causal, a, 0.0)
                if r0:
                    a_ref[:r0, cs] = jnp.zeros((r0, sub), BF16)
            a_ref[r0:, cs] = a.astype(BF16)
            carry_ref[r0:, :] = later[:, 0:1] - sp[:, 0:1]
        acc_ref[...] += _dot(a_ref[...], v_ref[...])

    @pl.when(kj == qi)
    def _():
        body(True)

    @pl.when(kj != qi)
    def _():
        body(False)

    @pl.when(kj == 0)
    def _():
        o_ref[...] = acc_ref[...].astype(o_ref.dtype)


def sb_attention(p_sb, tb=1024):
    T = p_sb.shape[0]
    tb = min(tb, T)
    nq = T // tb
    qi_l, kj_l = [], []
    for i in range(nq):
        for j in range(i, -1, -1):
            qi_l.append(i)
            kj_l.append(j)
    qi_tab = jnp.asarray(qi_l, jnp.int32)
    kj_tab = jnp.asarray(kj_l, jnp.int32)
    H = SB_HEADS
    grid_spec = pltpu.PrefetchScalarGridSpec(
        num_scalar_prefetch=2,
        grid=(H, len(qi_l)),
        in_specs=[pl.BlockSpec((tb, SB_DH), lambda h, s, qt, kt: (qt[s], h)),
                  pl.BlockSpec((tb, SB_DH), lambda h, s, qt, kt: (kt[s], H + h)),
                  pl.BlockSpec((tb, SB_DH), lambda h, s, qt, kt: (kt[s], 2 * H + h))],
        out_specs=pl.BlockSpec((tb, SB_DH), lambda h, s, qt, kt: (qt[s], h)),
        scratch_shapes=[pltpu.VMEM((tb, SB_DH), F32), pltpu.VMEM((tb, 1), F32), pltpu.VMEM((tb, tb), BF16)],
    )
    return pl.pallas_call(
        functools.partial(_sb_kernel, tb=tb),
        grid_spec=grid_spec,
        out_shape=jax.ShapeDtypeStruct((T, SB_W), BF16),
        compiler_params=_params("parallel", "arbitrary"),
        name="sb_attention",
    )(qi_tab, kj_tab, p_sb, p_sb, p_sb)


def _gla_kernel(q_ref, k_ref, v_ref, g_ref, ap_ref, a2_ref, ab_ref, ng_ref, o_ref, s_ref, *, tb):
    C = CHUNK

    @pl.when(pl.program_id(1) == 0)
    def _():
        s_ref[...] = jnp.zeros_like(s_ref)

    tt = lax.broadcasted_iota(jnp.int32, (C, C), 0)
    ss = lax.broadcasted_iota(jnp.int32, (C, C), 1)
    causal = ss <= tt
    tri = jnp.where(causal, 1.0, 0.0).astype(BF16)
    st = s_ref[...]
    for c in range(tb // C):
        sl = slice(c * C, (c + 1) * C)
        pre = _dot(ap_ref[sl, :].astype(BF16), a2_ref[...]) + ab_ref[...]
        la = -_softplus(-pre) * (1.0 / GLA_TAU)
        hi, lo = _split(la)
        b = _dot(tri, hi) + _dot(tri, lo)
        b_last = b[C - 1:C, :]
        k = k_ref[sl, :]
        q_in = (q_ref[sl, :] * (GLA_DK ** -0.5) * jnp.exp(b)).astype(BF16)
        k_in = (k * jnp.exp(-b)).astype(BF16)
        k_end = (k * jnp.exp(b_last - b)).astype(BF16)
        v = v_ref[sl, :].astype(BF16)
        scores = jnp.where(causal, _dot_nt(q_in, k_in), 0.0)
        o = _dot(scores.astype(BF16), v) + _dot_nt(q_in, st.astype(BF16))
        st = jnp.exp(b_last) * st + _dot_tn(v, k_end)
        o = o * lax.rsqrt(jnp.mean(o * o, axis=-1, keepdims=True) + EPS) * ng_ref[...]
        g = g_ref[sl, :]
        o_ref[sl, :] = (o * (g * _sigmoid(g))).astype(o_ref.dtype)
    s_ref[...] = st


def gla(p_gla, a2p, ab, norm_g, tb=512):
    T = p_gla.shape[0]
    tb = min(tb, T)
    H = GLA_HEADS
    kq = GLA_KW // GLA_DK
    kv = 2 * GLA_KW // GLA_DV
    ap_blk = (2 * GLA_KW + 2 * GLA_VW) // LANES
    return pl.pallas_call(
        functools.partial(_gla_kernel, tb=tb),
        grid=(H, T // tb),
        in_specs=[pl.BlockSpec((tb, GLA_DK), lambda h, i: (i, h)),
                  pl.BlockSpec((tb, GLA_DK), lambda h, i: (i, kq + h)),
                  pl.BlockSpec((tb, GLA_DV), lambda h, i: (i, kv + h)),
                  pl.BlockSpec((tb, GLA_DV), lambda h, i: (i, kv + H + h)),
                  pl.BlockSpec((tb, LANES), lambda h, i: (i, ap_blk)),
                  pl.BlockSpec((LANES, GLA_DK), lambda h, i: (0, h)),
                  pl.BlockSpec((1, GLA_DK), lambda h, i: (0, h)),
                  pl.BlockSpec((1, GLA_DV), lambda h, i: (0, 0))],
        out_specs=pl.BlockSpec((tb, GLA_DV), lambda h, i: (i, h)),
        out_shape=jax.ShapeDtypeStruct((T, GLA_VW), BF16),
        scratch_shapes=[pltpu.VMEM((GLA_DV, GLA_DK), F32)],
        compiler_params=_params("parallel", "arbitrary"),
        name="gla",
    )(p_gla, p_gla, p_gla, p_gla, p_gla, a2p, ab.reshape(1, GLA_KW), norm_g.reshape(1, GLA_DV))


def _wkv_kernel(r_ref, lw_ref, k_ref, v_ref, a_ref, b_ref, o_ref, s_ref, *, R, HB, NS):
    C = RW_CHUNK
    N = RW_DH
    nch = R // C

    @pl.when(pl.program_id(1) == 0)
    def _():
        s_ref[...] = jnp.zeros_like(s_ref)

    row = lax.broadcasted_iota(jnp.int32, (R, R), 0)
    col = lax.broadcasted_iota(jnp.int32, (R, R), 1)
    same = (row // C) == (col // C)
    m_incl = same & (col <= row)
    m_strict = same & (col < row)
    m_incl_t = same & (row <= col)
    one = lambda m: jnp.where(m, 1.0, 0.0).astype(BF16)
    cum_incl = one(m_incl)
    cum_all = one(same)
    eye_r = one(row == col)
    eye_rf = jnp.where(row == col, 1.0, 0.0)
    ri = lax.broadcasted_iota(jnp.int32, (N, N), 0)
    ci = lax.broadcasted_iota(jnp.int32, (N, N), 1)
    eye_n = one(ri == ci)
    lane_chunk = lax.broadcasted_iota(jnp.int32, (N, R), 1) // C
    steps = C.bit_length() - 2

    def prepare(ti, hd):
        rows = slice(ti * R, (ti + 1) * R)
        r, lw, k, v, a, b = (ref[hd, rows, :] for ref in (r_ref, lw_ref, k_ref, v_ref, a_ref, b_ref))
        lw_hi, lw_lo = _split(lw)
        g_in = _dot(cum_incl, lw_hi) + _dot(cum_incl, lw_lo)
        g_last = _dot(cum_all, lw_hi) + _dot(cum_all, lw_lo)
        yield
        r_d = (r * jnp.exp(g_in)).astype(BF16)
        a_d = (a * jnp.exp(g_in - lw)).astype(BF16)
        inv = jnp.exp(-g_in)
        b_h = (b * inv).astype(BF16)
        k_h = (k * inv).astype(BF16)
        to_end = jnp.exp(g_last - g_in)
        b_e = (b * to_end).astype(BF16)
        k_e = (k * to_end).astype(BF16)
        dec = jnp.exp(g_last)

        gram = _dot_nt(jnp.concatenate([a_d, r_d], axis=0), jnp.concatenate([b_h, k_h], axis=0))
        l_ab = jnp.where(m_strict, gram[:R, :R], 0.0)
        l_ak = jnp.where(m_strict, gram[:R, R:], 0.0).astype(BF16)
        m_rk = jnp.where(m_incl, gram[R:, R:], 0.0).astype(BF16)
        m_rb_t = jnp.where(m_incl_t, _dot_nt(b_h, r_d), 0.0).astype(BF16)
        xt = _dot_nt(eye_n, jnp.concatenate([v.astype(BF16), a_d, r_d], axis=0))
        v_t = xt[:, :R].astype(BF16)
        a_dt = xt[:, R:2 * R].astype(BF16)
        r_dt = xt[:, 2 * R:].astype(BF16)
        yield

        inv_m = eye_rf + l_ab
        p = l_ab
        lv_t = _dot_nt(v_t, l_ak).astype(BF16)
        o_loct = _dot_nt(v_t, m_rk)
        for _ in range(steps):
            pb = p.astype(BF16)
            p = _dot(pb, pb)
            yield
            inv_m = inv_m + _dot(inv_m.astype(BF16), p.astype(BF16))
            yield
        inv_b = inv_m.astype(BF16)
        w_at = _dot_nt(a_dt, inv_b).astype(BF16)
        u_vt = _dot_nt(lv_t, inv_b)
        yield
        stack = lambda x: jnp.concatenate(
            [jnp.where(lane_chunk == c, x, jnp.zeros_like(x)) for c in range(nch)], axis=0)
        p_c = _dot(stack(w_at), b_e).astype(BF16)
        b_c = _dot(jnp.concatenate([stack(u_vt.astype(BF16)), stack(v_t)], axis=1),
                   jnp.concatenate([b_e, k_e], axis=0))
        return dict(rhs=jnp.concatenate([w_at, r_dt], axis=1), u_vt=u_vt, dec=dec, p_c=p_c, b_c=b_c,
                    m_rb_t=m_rb_t, o_loct=o_loct)

    pre = {}
    st = [s_ref[hd] for hd in range(HB)]

    def scan(ti):
        entering = [[] for _ in range(HB)]
        for c in range(nch):
            for hd in range(HB):
                pr = pre[ti, hd]
                sb = st[hd].astype(BF16)
                entering[hd].append(sb)
                st[hd] = (pr['dec'][c * C:c * C + 1, :] * st[hd] + _dot(sb, pr['p_c'][c * N:(c + 1) * N, :])
                          + pr['b_c'][c * N:(c + 1) * N, :])
            yield
        for hd in range(HB):
            pr = pre[ti, hd]
            res = _dot(jnp.concatenate(entering[hd], axis=0), pr['rhs'])
            u_t = pr['u_vt']
            o_t = pr['o_loct']
            for c in range(nch):
                m = lane_chunk == c
                u_t = u_t + jnp.where(m, res[c * N:(c + 1) * N, :R], 0.0)
                o_t = o_t + jnp.where(m, res[c * N:(c + 1) * N, R:], 0.0)
            o_t = o_t + _dot(u_t.astype(BF16), pr['m_rb_t'])
            hi, lo = _split(o_t)
            o_ref[hd, ti * R:(ti + 1) * R, :] = _dot_nt(eye_r, hi) + _dot_nt(eye_r, lo)
            if hd % 4 == 3:
                yield

    def run(gens):
        while gens:
            for key in list(gens):
                try:
                    next(gens[key])
                except StopIteration as done:
                    pre[key] = done.value
                    del gens[key]

    run({(0, hd): prepare(0, hd) for hd in range(HB)})
    for ti in range(NS):
        gens = {('scan', ti): scan(ti)}
        if ti + 1 < NS:
            gens.update({(ti + 1, hd): prepare(ti + 1, hd) for hd in range(HB)})
        run(gens)
    for hd in range(HB):
        s_ref[hd] = st[hd]


def wkv7(r, lw, k, v, a, b, R=128, HB=16, NS=2):
    H, T, N = r.shape
    R = min(R, T)
    HB = min(HB, H)
    NS = min(NS, T // R)
    spec = pl.BlockSpec((HB, NS * R, N), lambda h, i: (h, i, 0))
    return pl.pallas_call(
        functools.partial(_wkv_kernel, R=R, HB=HB, NS=NS),
        grid=(H // HB, T // (NS * R)),
        in_specs=[spec] * 6,
        out_specs=spec,
        out_shape=jax.ShapeDtypeStruct((H, T, N), F32),
        scratch_shapes=[pltpu.VMEM((HB, N, N), F32)],
        compiler_params=_params("parallel", "arbitrary"),
        name="wkv7",
    )(r, lw, k, v, a, b)


def _shift(x):
    return jnp.pad(x, ((1, 0), (0, 0)))[:-1]


def _heads(x):
    T = x.shape[0]
    return x.reshape(T, RW_HEADS, RW_DH).transpose(1, 0, 2)


def rwkv_time_mix(p_rw, ranks, v_first, w):
    T = p_rw.shape[0]
    r_p, k_p, v_p = (p_rw[:, i * RW_W:(i + 1) * RW_W] for i in range(3))
    nlr = sum(ranks)
    low = p_rw[:, 3 * RW_W:3 * RW_W + nlr] + _shift(p_rw[:, 3 * RW_W + nlr:3 * RW_W + 2 * nlr])
    offs = [0]
    for rk in ranks:
        offs.append(offs[-1] + rk)
    low_w, low_a, low_g = (low[:, offs[i]:offs[i + 1]] for i in range(3))
    mu = w['mu_rkv']
    r = r_p + (_shift(r_p) - r_p) * mu[0]
    k = k_p + (_shift(k_p) - k_p) * mu[1]
    v = v_p + (_shift(v_p) - v_p) * mu[2]
    w_log = -jax.nn.softplus(-(w['w0'] + mm(jnp.tanh(low_w), w['w2']))) - 0.5
    log_decay = -jnp.exp(w_log)
    if v_first is None:
        v_first = v
    else:
        low_v = low[:, offs[3]:offs[4]]
        v = v + (v_first - v) * jax.nn.sigmoid(w['v0'] + mm(low_v, w['v2']))
    a = jax.nn.sigmoid(w['a0'] + mm(low_a, w['a2']))
    g = mm(jax.nn.sigmoid(low_g), w['g2'])

    kk = (k * w['k_k']).reshape(T, RW_HEADS, RW_DH)
    kk = (kk * lax.rsqrt(jnp.maximum(jnp.sum(kk * kk, axis=-1, keepdims=True), 1e-24))).reshape(T, RW_W)
    k = k * (1.0 + (a - 1.0) * w['k_a'])
    y = wkv7(_heads(r), _heads(log_decay), _heads(k), _heads(v), _heads(-kk), _heads(kk * a))
    y = y.transpose(1, 0, 2)
    mean = jnp.mean(y, axis=-1, keepdims=True)
    var = jnp.mean(jnp.square(y - mean), axis=-1, keepdims=True)
    y = ((y - mean) * lax.rsqrt(var + RW_LN_EPS)).reshape(T, RW_W) * w['ln_g'] + w['ln_b']
    rk = (r * k).reshape(T, RW_HEADS, RW_DH) * w['r_k']
    bonus = jnp.sum(rk, axis=-1, keepdims=True) * v.reshape(T, RW_HEADS, RW_DH)
    y = y + bonus.reshape(T, RW_W)
    return (y * g).astype(BF16), v_first


def _pad_cols(w, mult):
    n = w.shape[1]
    pad = (-n) % mult
    return w if pad == 0 else jnp.pad(w, ((0, 0), (0, pad)))


def kernel(x, mem, ffn1_pre, ffn1_post, ffn1_w1, ffn1_w3, ffn1_w2, mix_pre, mix_post, w_in, gla_a1, gla_a2, gla_ab, gla_norm, rw_mu_rkv, rw_mu_wag, rw_w0, rw_w1, rw_w2, rw_a0, rw_a1, rw_a2, rw_g1, rw_g2, rw_k_k, rw_k_a, rw_r_k, rw_ln_g, rw_ln_b, rw_mu_vl, rw_v0, rw_v1, rw_v2, branch_w, gate_a, gate_c, gate_b, w_out, xa_pre, xa_post, mem_norm, xa_wq, xa_wk, xa_wv, xa_wo, ffn2_pre, ffn2_post, ffn2_w1, ffn2_w3, ffn2_w2):
    B, T, D = x.shape
    depth = ffn1_pre.shape[0]
    bf = lambda t: t.astype(BF16)
    sb_end = 3 * SB_W
    gla_end = sb_end + 2 * GLA_KW + 2 * GLA_VW
    outs = []
    for bi in range(B):
        xs = x[bi]
        mem_b = mem[bi]
        h = rmsnorm(xs, ffn1_pre[0])
        v_first = None
        for l in range(depth):
            y = ffn(h, bf(ffn1_w1[l]), bf(ffn1_w3[l]), bf(ffn1_w2[l]))
            xs, h = resid_norm(xs, y, ffn1_post[l], mix_pre[l], 0.5)

            w_sb = jnp.concatenate([w_in[l][:, :SB_W] * SB_SCALE, w_in[l][:, SB_W:sb_end]], axis=1)
            p_sb = mm(h, bf(w_sb), BF16)
            w_gla = jnp.concatenate([w_in[l][:, sb_end:gla_end], _pad_cols(gla_a1[l], LANES)], axis=1)
            p_gla = mm(h, bf(w_gla))
            lows = [(rw_w1[l], rw_mu_wag[l, 0]), (rw_a1[l], rw_mu_wag[l, 1]), (rw_g1[l], rw_mu_wag[l, 2])]
            if l > 0:
                lows.append((rw_v1[l - 1], rw_mu_vl[l - 1]))
            ranks = [wl.shape[1] for wl, _ in lows]
            w_rw = jnp.concatenate([w_in[l][:, gla_end:]]
                                   + [wl * (1.0 - m)[:, None] for wl, m in lows]
                                   + [wl * m[:, None] for wl, m in lows], axis=1)
            p_rw = mm(h, bf(_pad_cols(w_rw, 512)))
            ha = mm(h, bf(jnp.concatenate([gate_a[l, g] for g in range(gate_a.shape[1])], axis=1)), BF16)

            y_sb = sb_attention(p_sb)
            a2p = jnp.pad(gla_a2[l], ((0, LANES - gla_a2.shape[1]), (0, 0)))
            y_gla = gla(p_gla, bf(a2p), gla_ab[l], gla_norm[l])
            rw_w = dict(mu_rkv=rw_mu_rkv[l], w0=rw_w0[l], w2=bf(rw_w2[l]), a0=rw_a0[l], a2=bf(rw_a2[l]),
                        g2=bf(rw_g2[l]), k_k=rw_k_k[l], k_a=rw_k_a[l], r_k=rw_r_k[l],
                        ln_g=rw_ln_g[l], ln_b=rw_ln_b[l])
            if l > 0:
                rw_w.update(v0=rw_v0[l - 1], v2=bf(rw_v2[l - 1]))
            y_rw, v_first = rwkv_time_mix(p_rw, ranks, v_first, rw_w)

            y = merge(ha, y_sb, y_gla, y_rw, bf(gate_c[l]), gate_b[l], bf(branch_w[l]), bf(w_out[l]))
            xs, h = resid_norm(xs, y, mix_post[l], xa_pre[l], 1.0)

            mem_n = rmsnorm(mem_b, mem_norm[l])
            kx = mm(mem_n, bf(xa_wk[l]), BF16)
            vx = mm(mem_n, bf(xa_wv[l]), BF16)
            y = xattn(h, bf(xa_wq[l]), kx, vx, bf(xa_wo[l]))
            xs, h = resid_norm(xs, y, xa_post[l], ffn2_pre[l], 1.0)

            y = ffn(h, bf(ffn2_w1[l]), bf(ffn2_w3[l]), bf(ffn2_w2[l]))
            g_next = ffn1_pre[l + 1] if l + 1 < depth else ffn1_pre[0]
            xs, h = resid_norm(xs, y, ffn2_post[l], g_next, 0.5)
        outs.append(xs)
    return jnp.stack(outs, axis=0)
```

```python
import functools
import math

import jax
import jax.numpy as jnp
from jax import lax
from jax.experimental import pallas as pl
from jax.experimental.pallas import tpu as pltpu

F32 = jnp.float32
BF16 = jnp.bfloat16

EPS = 1e-6
CHUNK = 64
SB_HEADS, SB_DH, SB_W = 4, 256, 1024
GLA_HEADS, GLA_DK, GLA_DV, GLA_KW, GLA_VW = 4, 128, 256, 512, 1024
GLA_TAU = 16.0
RW_HEADS, RW_DH, RW_W = 16, 64, 1024
RW_LN_EPS = 64e-5
RW_CHUNK = 16
XA_HEADS, XA_DH, XA_W = 4, 256, 1024
LANES = 128
VMEM_LIMIT = 56 * 1024 * 1024


def _params(*sem):
    return pltpu.CompilerParams(dimension_semantics=sem, vmem_limit_bytes=VMEM_LIMIT)


def _dot(a, b):
    return jnp.dot(a, b, preferred_element_type=F32)


def _dot_nt(a, b):
    return lax.dot_general(a, b, (((1,), (1,)), ((), ())), preferred_element_type=F32)


def _dot_tn(a, b):
    return lax.dot_general(a, b, (((0,), (0,)), ((), ())), preferred_element_type=F32)


def _split(x):
    hi = x.astype(BF16)
    lo = (x - hi.astype(F32)).astype(BF16)
    return hi, lo


def _softplus(z):
    return jnp.maximum(z, 0.0) + jnp.log1p(jnp.exp(-jnp.abs(z)))


def _sigmoid(z):
    return 1.0 / (1.0 + jnp.exp(-z))


def _tile(n, pref):
    t = min(n, pref)
    while n % t:
        t -= LANES
    return t


def _mm_kernel(a_ref, b_ref, o_ref):
    o_ref[...] = _dot(a_ref[...].astype(BF16), b_ref[...]).astype(o_ref.dtype)


def mm(a, b, out_dtype=F32, tm=1024, tn=512):
    M, K = a.shape
    N = b.shape[1]
    tm = min(tm, M)
    tn = _tile(N, tn)
    assert M % tm == 0 and N % tn == 0
    return pl.pallas_call(
        _mm_kernel,
        grid=(M // tm, N // tn),
        in_specs=[pl.BlockSpec((tm, K), lambda i, j: (i, 0)),
                  pl.BlockSpec((K, tn), lambda i, j: (0, j))],
        out_specs=pl.BlockSpec((tm, tn), lambda i, j: (i, j)),
        out_shape=jax.ShapeDtypeStruct((M, N), out_dtype),
        compiler_params=_params("parallel", "parallel"),
        name="mm",
    )(a, b)


def _rms(x, g):
    return x * lax.rsqrt(jnp.mean(x * x, axis=-1, keepdims=True) + EPS) * g


def _rmsnorm_kernel(x_ref, g_ref, o_ref):
    o_ref[...] = _rms(x_ref[...], g_ref[...]).astype(o_ref.dtype)


def rmsnorm(x, g, out_dtype=BF16, tm=256):
    M, D = x.shape
    tm = min(tm, M)
    return pl.pallas_call(
        _rmsnorm_kernel,
        grid=(M // tm,),
        in_specs=[pl.BlockSpec((tm, D), lambda i: (i, 0)),
                  pl.BlockSpec((1, D), lambda i: (0, 0))],
        out_specs=pl.BlockSpec((tm, D), lambda i: (i, 0)),
        out_shape=jax.ShapeDtypeStruct((M, D), out_dtype),
        compiler_params=_params("parallel"),
        name="rmsnorm",
    )(x, g.reshape(1, D))


def _resid_norm_kernel(x_ref, y_ref, gp_ref, gn_ref, xo_ref, h_ref, *, coef):
    xn = x_ref[...] + coef * _rms(y_ref[...], gp_ref[...])
    xo_ref[...] = xn
    h_ref[...] = _rms(xn, gn_ref[...]).astype(h_ref.dtype)


def resid_norm(x, y, g_post, g_next, coef, tm=256):
    M, D = x.shape
    tm = min(tm, M)
    row = pl.BlockSpec((tm, D), lambda i: (i, 0))
    vec = pl.BlockSpec((1, D), lambda i: (0, 0))
    return pl.pallas_call(
        functools.partial(_resid_norm_kernel, coef=coef),
        grid=(M // tm,),
        in_specs=[row, row, vec, vec],
        out_specs=[row, row],
        out_shape=[jax.ShapeDtypeStruct((M, D), F32), jax.ShapeDtypeStruct((M, D), BF16)],
        compiler_params=_params("parallel"),
        name="resid_norm",
    )(x, y, g_post.reshape(1, D), g_next.reshape(1, D))


def _ffn_kernel(h_ref, w1_ref, w3_ref, w2_ref, o_ref):
    h = h_ref[...]
    a = _dot(h, w1_ref[...])
    b = _dot(h, w3_ref[...])
    u = (a * _sigmoid(a) * b).astype(BF16)
    part = _dot(u, w2_ref[...])

    @pl.when(pl.program_id(1) == 0)
    def _():
        o_ref[...] = part

    @pl.when(pl.program_id(1) != 0)
    def _():
        o_ref[...] += part


def ffn(h, w1, w3, w2, tm=512, tf=512):
    M, D = h.shape
    Fd = w1.shape[1]
    tm = min(tm, M)
    tf = min(tf, Fd)
    return pl.pallas_call(
        _ffn_kernel,
        grid=(M // tm, Fd // tf),
        in_specs=[pl.BlockSpec((tm, D), lambda i, j: (i, 0)),
                  pl.BlockSpec((D, tf), lambda i, j: (0, j)),
                  pl.BlockSpec((D, tf), lambda i, j: (0, j)),
                  pl.BlockSpec((tf, D), lambda i, j: (j, 0))],
        out_specs=pl.BlockSpec((tm, D), lambda i, j: (i, 0)),
        out_shape=jax.ShapeDtypeStruct((M, D), F32),
        compiler_params=_params("parallel", "arbitrary"),
        name="ffn",
    )(h, w1, w3, w2)


def _merge_kernel(ha_ref, ysb_ref, ygl_ref, yrw_ref, gc_ref, gb_ref, bw_ref, wo_ref, o_ref, *, rank):
    ha = ha_ref[...]
    m = None
    for g, y_ref in enumerate((ysb_ref, ygl_ref, yrw_ref)):
        gate = _sigmoid(_dot(ha[:, g * rank:(g + 1) * rank], gc_ref[g]) + gb_ref[g])
        term = gate * _dot(y_ref[...], bw_ref[g])
        m = term if m is None else m + term
    part = _dot(m.astype(BF16), wo_ref[...])

    @pl.when(pl.program_id(1) == 0)
    def _():
        o_ref[...] = part

    @pl.when(pl.program_id(1) != 0)
    def _():
        o_ref[...] += part


def merge(ha, y_sb, y_gla, y_rw, gate_c, gate_b, branch_w, w_out, tm=512, tn=512):
    M = ha.shape[0]
    nb, rank, D = gate_c.shape
    W = branch_w.shape[1]
    tm = min(tm, M)
    tn = min(tn, D)
    ybs = pl.BlockSpec((tm, W), lambda i, j: (i, 0))
    return pl.pallas_call(
        functools.partial(_merge_kernel, rank=rank),
        grid=(M // tm, D // tn),
        in_specs=[pl.BlockSpec((tm, nb * rank), lambda i, j: (i, 0)), ybs, ybs, ybs,
                  pl.BlockSpec((nb, rank, tn), lambda i, j: (0, 0, j)),
                  pl.BlockSpec((nb, 1, tn), lambda i, j: (0, 0, j)),
                  pl.BlockSpec((nb, W, tn), lambda i, j: (0, 0, j)),
                  pl.BlockSpec((tn, D), lambda i, j: (j, 0))],
        out_specs=pl.BlockSpec((tm, D), lambda i, j: (i, 0)),
        out_shape=jax.ShapeDtypeStruct((M, D), F32),
        compiler_params=_params("parallel", "arbitrary"),
        name="merge",
    )(ha, y_sb, y_gla, y_rw, gate_c, gate_b.reshape(nb, 1, D), branch_w, w_out)


def _xattn_kernel(h_ref, wq_ref, k_ref, v_ref, wo_ref, o_ref):
    q = _dot(h_ref[...], wq_ref[...])
    outs = []
    for hh in range(XA_HEADS):
        sl = slice(hh * XA_DH, (hh + 1) * XA_DH)
        s = _dot_nt(q[:, sl].astype(BF16), k_ref[:, sl]) * (XA_DH ** -0.5)
        e = jnp.exp(s - jnp.max(s, axis=-1, keepdims=True))
        p = e / jnp.sum(e, axis=-1, keepdims=True)
        outs.append(_dot(p.astype(BF16), v_ref[:, sl]).astype(BF16))
    o_ref[...] = _dot(jnp.concatenate(outs, axis=1), wo_ref[...])


def xattn(h, wq, k, v, wo, tm=256):
    M, D = h.shape
    nm = k.shape[0]
    tm = min(tm, M)
    return pl.pallas_call(
        _xattn_kernel,
        grid=(M // tm,),
        in_specs=[pl.BlockSpec((tm, D), lambda i: (i, 0)),
                  pl.BlockSpec((D, XA_W), lambda i: (0, 0)),
                  pl.BlockSpec((nm, XA_W), lambda i: (0, 0)),
                  pl.BlockSpec((nm, XA_W), lambda i: (0, 0)),
                  pl.BlockSpec((XA_W, D), lambda i: (0, 0))],
        out_specs=pl.BlockSpec((tm, D), lambda i: (i, 0)),
        out_shape=jax.ShapeDtypeStruct((M, D), F32),
        compiler_params=_params("parallel"),
        name="xattn",
    )(h, wq, k, v, wo)


SB_SCALE = SB_DH ** -0.5
assert math.frexp(SB_SCALE)[0] == 0.5, "q pre-scaling is only exact for a power-of-two scale"
SB_SUB = 256


def _sb_kernel(qi_tab, kj_tab, q_ref, k_ref, v_ref, o_ref, acc_ref, carry_ref, a_ref, *, tb):
    step = pl.program_id(1)
    qi = qi_tab[step]
    kj = kj_tab[step]
    sub = min(SB_SUB, tb)
    nsub = tb // sub

    @pl.when(kj == qi)
    def _():
        acc_ref[...] = jnp.zeros_like(acc_ref)
        carry_ref[...] = jnp.zeros_like(carry_ref)

    def body(diag):
        z_full = _dot_nt(q_ref[...], k_ref[...])
        jj = lax.broadcasted_iota(jnp.int32, (sub, sub), 0)
        ss = lax.broadcasted_iota(jnp.int32, (sub, sub), 1)
        neg_tri = jnp.where(jj > ss, -1.0, 0.0).astype(BF16)
        for c in reversed(range(nsub)):
            r0 = c * sub if diag else 0
            cs = slice(c * sub, (c + 1) * sub)
            z = z_full[r0:, cs]
            sp = jnp.maximum(z, 0.0) + jnp.log(1.0 + jnp.exp(-jnp.abs(z)))
            if diag:
                row = lax.broadcasted_iota(jnp.int32, z.shape, 0)
                col = lax.broadcasted_iota(jnp.int32, z.shape, 1)
                causal = col < row + (r0 - c * sub)
                sp = jnp.where(causal, sp, 0.0)
            later = _dot(sp.astype(BF16), neg_tri) + carry_ref[r0:, :]
            a = jnp.exp(z - sp + later)
            if diag:
                a = jnp.where(causal, a, 0.0)
                if r0:
                    a_ref[:r0, cs] = jnp.zeros((r0, sub), BF16)
            a_ref[r0:, cs] = a.astype(BF16)
            carry_ref[r0:, :] = later[:, 0:1] - sp[:, 0:1]
        acc_ref[...] += _dot(a_ref[...], v_ref[...])

    @pl.when(kj == qi)
    def _():
        body(True)

    @pl.when(kj != qi)
    def _():
        body(False)

    @pl.when(kj == 0)
    def _():
        o_ref[...] = acc_ref[...].astype(o_ref.dtype)


def sb_attention(p_sb, tb=1024):
    T = p_sb.shape[0]
    tb = min(tb, T)
    nq = T // tb
    qi_l, kj_l = [], []
    for i in range(nq):
        for j in range(i, -1, -1):
            qi_l.append(i)
            kj_l.append(j)
    qi_tab = jnp.asarray(qi_l, jnp.int32)
    kj_tab = jnp.asarray(kj_l, jnp.int32)
    H = SB_HEADS
    grid_spec = pltpu.PrefetchScalarGridSpec(
        num_scalar_prefetch=2,
        grid=(H, len(qi_l)),
        in_specs=[pl.BlockSpec((tb, SB_DH), lambda h, s, qt, kt: (qt[s], h)),
                  pl.BlockSpec((tb, SB_DH), lambda h, s, qt, kt: (kt[s], H + h)),
                  pl.BlockSpec((tb, SB_DH), lambda h, s, qt, kt: (kt[s], 2 * H + h))],
        out_specs=pl.BlockSpec((tb, SB_DH), lambda h, s, qt, kt: (qt[s], h)),
        scratch_shapes=[pltpu.VMEM((tb, SB_DH), F32), pltpu.VMEM((tb, 1), F32), pltpu.VMEM((tb, tb), BF16)],
    )
    return pl.pallas_call(
        functools.partial(_sb_kernel, tb=tb),
        grid_spec=grid_spec,
        out_shape=jax.ShapeDtypeStruct((T, SB_W), BF16),
        compiler_params=_params("parallel", "arbitrary"),
        name="sb_attention",
    )(qi_tab, kj_tab, p_sb, p_sb, p_sb)


def _gla_kernel(q_ref, k_ref, v_ref, g_ref, ap_ref, a2_ref, ab_ref, ng_ref, o_ref, s_ref, *, tb):
    C = CHUNK

    @pl.when(pl.program_id(1) == 0)
    def _():
        s_ref[...] = jnp.zeros_like(s_ref)

    tt = lax.broadcasted_iota(jnp.int32, (C, C), 0)
    ss = lax.broadcasted_iota(jnp.int32, (C, C), 1)
    causal = ss <= tt
    tri = jnp.where(causal, 1.0, 0.0).astype(BF16)
    st = s_ref[...]
    for c in range(tb // C):
        sl = slice(c * C, (c + 1) * C)
        pre = _dot(ap_ref[sl, :].astype(BF16), a2_ref[...]) + ab_ref[...]
        la = -_softplus(-pre) * (1.0 / GLA_TAU)
        hi, lo = _split(la)
        b = _dot(tri, hi) + _dot(tri, lo)
        b_last = b[C - 1:C, :]
        k = k_ref[sl, :]
        q_in = (q_ref[sl, :] * (GLA_DK ** -0.5) * jnp.exp(b)).astype(BF16)
        k_in = (k * jnp.exp(-b)).astype(BF16)
        k_end = (k * jnp.exp(b_last - b)).astype(BF16)
        v = v_ref[sl, :].astype(BF16)
        scores = jnp.where(causal, _dot_nt(q_in, k_in), 0.0)
        o = _dot(scores.astype(BF16), v) + _dot_nt(q_in, st.astype(BF16))
        st = jnp.exp(b_last) * st + _dot_tn(v, k_end)
        o = o * lax.rsqrt(jnp.mean(o * o, axis=-1, keepdims=True) + EPS) * ng_ref[...]
        g = g_ref[sl, :]
        o_ref[sl, :] = (o * (g * _sigmoid(g))).astype(o_ref.dtype)
    s_ref[...] = st


def gla(p_gla, p_ap, ap_blk, a2p, ab, norm_g, tb=512):
    T = p_gla.shape[0]
    tb = min(tb, T)
    H = GLA_HEADS
    kq = GLA_KW // GLA_DK
    kv = 2 * GLA_KW // GLA_DV
    return pl.pallas_call(
        functools.partial(_gla_kernel, tb=tb),
        grid=(H, T // tb),
        in_specs=[pl.BlockSpec((tb, GLA_DK), lambda h, i: (i, h)),
                  pl.BlockSpec((tb, GLA_DK), lambda h, i: (i, kq + h)),
                  pl.BlockSpec((tb, GLA_DV), lambda h, i: (i, kv + h)),
                  pl.BlockSpec((tb, GLA_DV), lambda h, i: (i, kv + H + h)),
                  pl.BlockSpec((tb, LANES), lambda h, i: (i, ap_blk)),
                  pl.BlockSpec((LANES, GLA_DK), lambda h, i: (0, h)),
                  pl.BlockSpec((1, GLA_DK), lambda h, i: (0, h)),
                  pl.BlockSpec((1, GLA_DV), lambda h, i: (0, 0))],
        out_specs=pl.BlockSpec((tb, GLA_DV), lambda h, i: (i, h)),
        out_shape=jax.ShapeDtypeStruct((T, GLA_VW), BF16),
        scratch_shapes=[pltpu.VMEM((GLA_DV, GLA_DK), F32)],
        compiler_params=_params("parallel", "arbitrary"),
        name="gla",
    )(p_gla, p_gla, p_gla, p_gla, p_ap, a2p, ab.reshape(1, GLA_KW), norm_g.reshape(1, GLA_DV))


def _wkv_kernel(rd_ref, ad_ref, bh_ref, kh_ref, be_ref, ke_ref, v_ref, dec_ref, o_ref, s_ref, *, R, NS):
    C = RW_CHUNK
    N = RW_DH
    HB = RW_HEADS
    nch = R // C

    @pl.when(pl.program_id(0) == 0)
    def _():
        s_ref[...] = jnp.zeros_like(s_ref)

    row = lax.broadcasted_iota(jnp.int32, (R, R), 0)
    col = lax.broadcasted_iota(jnp.int32, (R, R), 1)
    same = (row // C) == (col // C)
    m_incl = same & (col <= row)
    m_strict = same & (col < row)
    m_incl_t = same & (row <= col)
    one = lambda m: jnp.where(m, 1.0, 0.0).astype(BF16)
    eye_r = one(row == col)
    eye_rf = jnp.where(row == col, 1.0, 0.0)
    ri = lax.broadcasted_iota(jnp.int32, (N, N), 0)
    ci = lax.broadcasted_iota(jnp.int32, (N, N), 1)
    eye_n = one(ri == ci)
    lane_chunk = lax.broadcasted_iota(jnp.int32, (N, R), 1) // C
    steps = C.bit_length() - 2

    def prepare(ti, hd):
        rows = slice(ti * R, (ti + 1) * R)
        lanes = slice(hd * N, (hd + 1) * N)
        r_d, a_d, b_h, k_h, b_e, k_e, v = (ref[rows, lanes] for ref in
                                           (rd_ref, ad_ref, bh_ref, kh_ref, be_ref, ke_ref, v_ref))
        gram = _dot_nt(jnp.concatenate([a_d, r_d], axis=0), jnp.concatenate([b_h, k_h], axis=0))
        l_ab = jnp.where(m_strict, gram[:R, :R], 0.0)
        l_ak = jnp.where(m_strict, gram[:R, R:], 0.0).astype(BF16)
        m_rk = jnp.where(m_incl, gram[R:, R:], 0.0).astype(BF16)
        m_rb_t = jnp.where(m_incl_t, _dot_nt(b_h, r_d), 0.0).astype(BF16)
        xt = _dot_nt(eye_n, jnp.concatenate([v, a_d, r_d], axis=0))
        v_t = xt[:, :R].astype(BF16)
        a_dt = xt[:, R:2 * R].astype(BF16)
        r_dt = xt[:, 2 * R:].astype(BF16)
        yield

        inv_m = eye_rf + l_ab
        p = l_ab
        lv_t = _dot_nt(v_t, l_ak).astype(BF16)
        o_loct = _dot_nt(v_t, m_rk)
        for _ in range(steps):
            pb = p.astype(BF16)
            p = _dot(pb, pb)
            yield
            inv_m = inv_m + _dot(inv_m.astype(BF16), p.astype(BF16))
            yield
        inv_b = inv_m.astype(BF16)
        w_at = _dot_nt(a_dt, inv_b).astype(BF16)
        u_vt = _dot_nt(lv_t, inv_b)
        yield
        stack = lambda x: jnp.concatenate(
            [jnp.where(lane_chunk == c, x, jnp.zeros_like(x)) for c in range(nch)], axis=0)
        p_c = _dot(stack(w_at), b_e).astype(BF16)
        b_c = _dot(jnp.concatenate([stack(u_vt.astype(BF16)), stack(v_t)], axis=1),
                   jnp.concatenate([b_e, k_e], axis=0))
        return dict(rhs=jnp.concatenate([w_at, r_dt], axis=1), u_vt=u_vt, p_c=p_c, b_c=b_c,
                    m_rb_t=m_rb_t, o_loct=o_loct)

    pre = {}
    st = [s_ref[hd] for hd in range(HB)]

    def scan(ti):
        entering = [[] for _ in range(HB)]
        for c in range(nch):
            for hd in range(HB):
                pr = pre[ti, hd]
                sb = st[hd].astype(BF16)
                entering[hd].append(sb)
                dec = dec_ref[ti * R + c * C:ti * R + c * C + 1, hd * N:(hd + 1) * N]
                st[hd] = dec * st[hd] + _dot(sb, pr['p_c'][c * N:(c + 1) * N, :]) + pr['b_c'][c * N:(c + 1) * N, :]
            yield
        for hd in range(HB):
            pr = pre[ti, hd]
            res = _dot(jnp.concatenate(entering[hd], axis=0), pr['rhs'])
            u_t = pr['u_vt']
            o_t = pr['o_loct']
            for c in range(nch):
                m = lane_chunk == c
                u_t = u_t + jnp.where(m, res[c * N:(c + 1) * N, :R], 0.0)
                o_t = o_t + jnp.where(m, res[c * N:(c + 1) * N, R:], 0.0)
            o_t = o_t + _dot(u_t.astype(BF16), pr['m_rb_t'])
            hi, lo = _split(o_t)
            o_ref[ti * R:(ti + 1) * R, hd * N:(hd + 1) * N] = _dot_nt(eye_r, hi) + _dot_nt(eye_r, lo)
            if hd % 4 == 3:
                yield

    def run(gens):
        while gens:
            for key in list(gens):
                try:
                    next(gens[key])
                except StopIteration as done:
                    pre[key] = done.value
                    del gens[key]

    run({(0, hd): prepare(0, hd) for hd in range(HB)})
    for ti in range(NS):
        gens = {('scan', ti): scan(ti)}
        if ti + 1 < NS:
            gens.update({(ti + 1, hd): prepare(ti + 1, hd) for hd in range(HB)})
        run(gens)
    for hd in range(HB):
        s_ref[hd] = st[hd]


def wkv7(rd, ad, bh, kh, be, ke, vb, dec, R=128, NS=2):
    T = rd.shape[0]
    R = min(R, T)
    NS = min(NS, T // R)
    spec = pl.BlockSpec((NS * R, RW_W), lambda i: (i, 0))
    return pl.pallas_call(
        functools.partial(_wkv_kernel, R=R, NS=NS),
        grid=(T // (NS * R),),
        in_specs=[spec] * 8,
        out_specs=spec,
        out_shape=jax.ShapeDtypeStruct((T, RW_W), F32),
        scratch_shapes=[pltpu.VMEM((RW_HEADS, RW_DH, RW_DH), F32)],
        compiler_params=_params("arbitrary"),
        name="wkv7",
    )(rd, ad, bh, kh, be, ke, vb, dec)


RW_LOW = (128, 128, 512, 128)
RW_VEC_ROWS = 16
HALO = 8


def _head_sum(x, bd):
    m = x.shape[0]
    hi = x.astype(BF16)
    r1 = x - hi.astype(F32)
    mid = r1.astype(BF16)
    lo = (r1 - mid.astype(F32)).astype(BF16)
    pieces = jnp.concatenate([hi, mid, lo], axis=0)
    outs = []
    for j in range(RW_W // LANES):
        s = _dot(pieces[:, j * LANES:(j + 1) * LANES], bd)
        outs.append(s[:m] + s[m:2 * m] + s[2 * m:])
    return jnp.concatenate(outs, axis=1)


def _head_ones():
    ri = lax.broadcasted_iota(jnp.int32, (LANES, LANES), 0)
    ci = lax.broadcasted_iota(jnp.int32, (LANES, LANES), 1)
    return jnp.where((ri // RW_DH) == (ci // RW_DH), 1.0, 0.0).astype(BF16)


def _rwkv_prep_kernel(*refs, tm, first):
    if first:
        (p_ref, halo_ref, vec_ref, w2_ref, a2_ref, g2_ref,
         rd_ref, ad_ref, bh_ref, kh_ref, be_ref, ke_ref, vb_ref, dec_ref, g_ref, bonus_ref, vf_out_ref) = refs
    else:
        (p_ref, halo_ref, vf_ref, vec_ref, w2_ref, a2_ref, g2_ref, v2_ref,
         rd_ref, ad_ref, bh_ref, kh_ref, be_ref, ke_ref, vb_ref, dec_ref, g_ref, bonus_ref) = refs
    C = RW_CHUNK
    W = RW_W
    slots = RW_LOW[:3] if first else RW_LOW
    nlr = sum(slots)
    has_prev = (pl.program_id(0) > 0).astype(F32)
    row1 = lax.broadcasted_iota(jnp.int32, (tm, 1), 0)

    def shifted(c0, c1):
        prev_last = halo_ref[HALO - 1:HALO, c0:c1] * has_prev
        return jnp.where(row1 == 0, prev_last, pltpu.roll(p_ref[:, c0:c1], 1, axis=0))

    mu_r, mu_k, mu_v, w0, a0, k_k, k_a, r_k, v0 = (vec_ref[i:i + 1, :] for i in range(9))
    r_p, k_p, v_p = (p_ref[:, i * W:(i + 1) * W] for i in range(3))
    r = r_p + (shifted(0, W) - r_p) * mu_r
    k = k_p + (shifted(W, 2 * W) - k_p) * mu_k
    v = v_p + (shifted(2 * W, 3 * W) - v_p) * mu_v
    low = p_ref[:, 3 * W:3 * W + nlr] + shifted(3 * W + nlr, 3 * W + 2 * nlr)
    o_w, o_a, o_g, o_v = 0, slots[0], slots[0] + slots[1], slots[0] + slots[1] + slots[2]
    w_log = -_softplus(-(w0 + _dot(jnp.tanh(low[:, o_w:o_a]).astype(BF16), w2_ref[...]))) - 0.5
    lw = -jnp.exp(w_log)
    a = _sigmoid(a0 + _dot(low[:, o_a:o_g].astype(BF16), a2_ref[...]))
    g_ref[...] = _dot(_sigmoid(low[:, o_g:o_v]).astype(BF16), g2_ref[...])
    if first:
        vf_out_ref[...] = v
    else:
        v = v + (vf_ref[...] - v) * _sigmoid(v0 + _dot(low[:, o_v:nlr].astype(BF16), v2_ref[...]))

    bd = _head_ones()
    kk = k * k_k
    kk = kk * lax.rsqrt(jnp.maximum(_head_sum(kk * kk, bd), 1e-24))
    k = k * (1.0 + (a - 1.0) * k_a)
    bonus_ref[...] = _head_sum(r * k * r_k, bd) * v
    b_vec = kk * a

    rr = lax.broadcasted_iota(jnp.int32, (tm, tm), 0)
    cc = lax.broadcasted_iota(jnp.int32, (tm, tm), 1)
    same = (rr // C) == (cc // C)
    cum_incl = jnp.where(same & (cc <= rr), 1.0, 0.0).astype(BF16)
    cum_all = jnp.where(same, 1.0, 0.0).astype(BF16)
    lw_hi, lw_lo = _split(lw)
    g_in = _dot(cum_incl, lw_hi) + _dot(cum_incl, lw_lo)
    g_last = _dot(cum_all, lw_hi) + _dot(cum_all, lw_lo)
    rd_ref[...] = (r * jnp.exp(g_in)).astype(BF16)
    ad_ref[...] = (-kk * jnp.exp(g_in - lw)).astype(BF16)
    inv = jnp.exp(-g_in)
    bh_ref[...] = (b_vec * inv).astype(BF16)
    kh_ref[...] = (k * inv).astype(BF16)
    to_end = jnp.exp(g_last - g_in)
    be_ref[...] = (b_vec * to_end).astype(BF16)
    ke_ref[...] = (k * to_end).astype(BF16)
    vb_ref[...] = v.astype(BF16)
    dec_ref[...] = jnp.exp(g_last)


def rwkv_prep(p_rw, v_first, vecs, w2, a2, g2, v2, tm=256):
    T, PW = p_rw.shape
    tm = min(tm, T)
    first = v_first is None
    row = lambda n: pl.BlockSpec((tm, n), lambda i: (i, 0))
    full = lambda arr: pl.BlockSpec(arr.shape, lambda i: (0, 0))
    halo = pl.BlockSpec((HALO, PW), lambda i: (jnp.maximum(i * (tm // HALO) - 1, 0), 0))
    ins = [p_rw, p_rw] + ([] if first else [v_first]) + [vecs, w2, a2, g2] + ([] if first else [v2])
    in_specs = ([row(PW), halo] + ([] if first else [row(RW_W)]) + [full(vecs), full(w2), full(a2), full(g2)]
                + ([] if first else [full(v2)]))
    n_bf, n_f32 = 7, (4 if first else 3)
    outs = pl.pallas_call(
        functools.partial(_rwkv_prep_kernel, tm=tm, first=first),
        grid=(T // tm,),
        in_specs=in_specs,
        out_specs=[row(RW_W)] * (n_bf + n_f32),
        out_shape=[jax.ShapeDtypeStruct((T, RW_W), BF16)] * n_bf + [jax.ShapeDtypeStruct((T, RW_W), F32)] * n_f32,
        compiler_params=_params("parallel"),
        name="rwkv_prep",
    )(*ins)
    return outs


def _rwkv_post_kernel(o_ref, g_ref, bonus_ref, lng_ref, lnb_ref, y_ref):
    bd = _head_ones()
    o = o_ref[...]
    d = o - _head_sum(o, bd) * (1.0 / RW_DH)
    var = _head_sum(d * d, bd) * (1.0 / RW_DH)
    y = d * lax.rsqrt(var + RW_LN_EPS) * lng_ref[...] + lnb_ref[...] + bonus_ref[...]
    y_ref[...] = (y * g_ref[...]).astype(y_ref.dtype)


def rwkv_post(o, g, bonus, ln_g, ln_b, tm=256):
    T = o.shape[0]
    tm = min(tm, T)
    row = pl.BlockSpec((tm, RW_W), lambda i: (i, 0))
    vec = pl.BlockSpec((1, RW_W), lambda i: (0, 0))
    return pl.pallas_call(
        _rwkv_post_kernel,
        grid=(T // tm,),
        in_specs=[row, row, row, vec, vec],
        out_specs=row,
        out_shape=jax.ShapeDtypeStruct((T, RW_W), BF16),
        compiler_params=_params("parallel"),
        name="rwkv_post",
    )(o, g, bonus, ln_g.reshape(1, RW_W), ln_b.reshape(1, RW_W))


def rwkv_time_mix(p_rw, v_first, vecs, w2, a2, g2, v2, ln_g, ln_b):
    outs = rwkv_prep(p_rw, v_first, vecs, w2, a2, g2, v2)
    rd, ad, bh, kh, be, ke, vb, dec, g, bonus = outs[:10]
    if v_first is None:
        v_first = outs[10]
    o = wkv7(rd, ad, bh, kh, be, ke, vb, dec)
    return rwkv_post(o, g, bonus, ln_g, ln_b), v_first


def _pad_cols(w, mult):
    n = w.shape[1]
    pad = (-n) % mult
    return w if pad == 0 else jnp.pad(w, ((0, 0), (0, pad)))


def kernel(x, mem, ffn1_pre, ffn1_post, ffn1_w1, ffn1_w3, ffn1_w2, mix_pre, mix_post, w_in, gla_a1, gla_a2, gla_ab, gla_norm, rw_mu_rkv, rw_mu_wag, rw_w0, rw_w1, rw_w2, rw_a0, rw_a1, rw_a2, rw_g1, rw_g2, rw_k_k, rw_k_a, rw_r_k, rw_ln_g, rw_ln_b, rw_mu_vl, rw_v0, rw_v1, rw_v2, branch_w, gate_a, gate_c, gate_b, w_out, xa_pre, xa_post, mem_norm, xa_wq, xa_wk, xa_wv, xa_wo, ffn2_pre, ffn2_post, ffn2_w1, ffn2_w3, ffn2_w2):
    B, T, D = x.shape
    depth = ffn1_pre.shape[0]
    bf = lambda t: t.astype(BF16)
    sb_end = 3 * SB_W
    gla_end = sb_end + 2 * GLA_KW + 2 * GLA_VW
    outs = []
    for bi in range(B):
        xs = x[bi]
        mem_b = mem[bi]
        h = rmsnorm(xs, ffn1_pre[0])
        v_first = None
        for l in range(depth):
            y = ffn(h, bf(ffn1_w1[l]), bf(ffn1_w3[l]), bf(ffn1_w2[l]))
            xs, h = resid_norm(xs, y, ffn1_post[l], mix_pre[l], 0.5)

            w_sb = jnp.concatenate([w_in[l][:, :SB_W] * SB_SCALE, w_in[l][:, SB_W:sb_end]], axis=1)
            p_sb = mm(h, bf(w_sb), BF16)
            p_gla = mm(h, bf(w_in[l][:, sb_end:gla_end]))
            lows = [(rw_w1[l], rw_mu_wag[l, 0]), (rw_a1[l], rw_mu_wag[l, 1]), (rw_g1[l], rw_mu_wag[l, 2])]
            if l > 0:
                lows.append((rw_v1[l - 1], rw_mu_vl[l - 1]))
            slot = lambda wl, i: jnp.pad(wl, ((0, 0), (0, RW_LOW[i] - wl.shape[1])))
            w_rw = jnp.concatenate([w_in[l][:, gla_end:]]
                                   + [slot(wl * (1.0 - m)[:, None], i) for i, (wl, m) in enumerate(lows)]
                                   + [slot(wl * m[:, None], i) for i, (wl, m) in enumerate(lows)], axis=1)
            gla_ap_col = w_rw.shape[1]
            w_rw = jnp.concatenate([w_rw, _pad_cols(gla_a1[l], LANES)], axis=1)
            p_rw = mm(h, bf(_pad_cols(w_rw, 512)))
            ha = mm(h, bf(jnp.concatenate([gate_a[l, g] for g in range(gate_a.shape[1])], axis=1)), BF16)

            y_sb = sb_attention(p_sb)
            a2p = jnp.pad(gla_a2[l], ((0, LANES - gla_a2.shape[1]), (0, 0)))
            y_gla = gla(p_gla, p_rw, gla_ap_col // LANES, bf(a2p), gla_ab[l], gla_norm[l])
            v0 = rw_v0[l - 1] if l > 0 else jnp.zeros((RW_W,), F32)
            vec_rows = [rw_mu_rkv[l, 0], rw_mu_rkv[l, 1], rw_mu_rkv[l, 2], rw_w0[l], rw_a0[l], rw_k_k[l], rw_k_a[l],
                        rw_r_k[l].reshape(RW_W), v0]
            vecs = jnp.pad(jnp.stack(vec_rows, axis=0), ((0, RW_VEC_ROWS - len(vec_rows)), (0, 0)))
            pad_rows = lambda w2d, i: jnp.pad(w2d, ((0, RW_LOW[i] - w2d.shape[0]), (0, 0)))
            v2 = bf(pad_rows(rw_v2[l - 1], 3)) if l > 0 else None
            y_rw, v_first = rwkv_time_mix(p_rw, v_first, vecs, bf(pad_rows(rw_w2[l], 0)), bf(pad_rows(rw_a2[l], 1)),
                                          bf(pad_rows(rw_g2[l], 2)), v2, rw_ln_g[l], rw_ln_b[l])

            y = merge(ha, y_sb, y_gla, y_rw, bf(gate_c[l]), gate_b[l], bf(branch_w[l]), bf(w_out[l]))
            xs, h = resid_norm(xs, y, mix_post[l], xa_pre[l], 1.0)

            mem_n = rmsnorm(mem_b, mem_norm[l])
            kx = mm(mem_n, bf(xa_wk[l]), BF16)
            vx = mm(mem_n, bf(xa_wv[l]), BF16)
            y = xattn(h, bf(xa_wq[l]), kx, vx, bf(xa_wo[l]))
            xs, h = resid_norm(xs, y, xa_post[l], ffn2_pre[l], 1.0)

            y = ffn(h, bf(ffn2_w1[l]), bf(ffn2_w3[l]), bf(ffn2_w2[l]))
            g_next = ffn1_pre[l + 1] if l + 1 < depth else ffn1_pre[0]
            xs, h = resid_norm(xs, y, ffn2_post[l], g_next, 0.5)
        outs.append(xs)
    return jnp.stack(outs, axis=0)
```

```python
import functools
import math

import jax
import jax.numpy as jnp
from jax import lax
from jax.experimental import pallas as pl
from jax.experimental.pallas import tpu as pltpu

F32 = jnp.float32
BF16 = jnp.bfloat16

EPS = 1e-6
CHUNK = 64
SB_HEADS, SB_DH, SB_W = 4, 256, 1024
GLA_HEADS, GLA_DK, GLA_DV, GLA_KW, GLA_VW = 4, 128, 256, 512, 1024
GLA_TAU = 16.0
RW_HEADS, RW_DH, RW_W = 16, 64, 1024
RW_LN_EPS = 64e-5
RW_CHUNK = 16
XA_HEADS, XA_DH, XA_W = 4, 256, 1024
LANES = 128
VMEM_LIMIT = 56 * 1024 * 1024


def _params(*sem):
    return pltpu.CompilerParams(dimension_semantics=sem, vmem_limit_bytes=VMEM_LIMIT)


def _dot(a, b):
    return jnp.dot(a, b, preferred_element_type=F32)


def _dot_nt(a, b):
    return lax.dot_general(a, b, (((1,), (1,)), ((), ())), preferred_element_type=F32)


def _dot_tn(a, b):
    return lax.dot_general(a, b, (((0,), (0,)), ((), ())), preferred_element_type=F32)


def _split(x):
    hi = x.astype(BF16)
    lo = (x - hi.astype(F32)).astype(BF16)
    return hi, lo


def _softplus(z):
    return jnp.maximum(z, 0.0) + jnp.log1p(jnp.exp(-jnp.abs(z)))


def _sigmoid(z):
    return 1.0 / (1.0 + jnp.exp(-z))


def _tile(n, pref):
    t = min(n, pref)
    while n % t:
        t -= LANES
    return t


def _mm_kernel(a_ref, b_ref, o_ref):
    o_ref[...] = _dot(a_ref[...].astype(BF16), b_ref[...]).astype(o_ref.dtype)


def mm(a, b, out_dtype=F32, tm=1024, tn=512):
    M, K = a.shape
    N = b.shape[1]
    tm = min(tm, M)
    tn = _tile(N, tn)
    assert M % tm == 0 and N % tn == 0
    return pl.pallas_call(
        _mm_kernel,
        grid=(M // tm, N // tn),
        in_specs=[pl.BlockSpec((tm, K), lambda i, j: (i, 0)),
                  pl.BlockSpec((K, tn), lambda i, j: (0, j))],
        out_specs=pl.BlockSpec((tm, tn), lambda i, j: (i, j)),
        out_shape=jax.ShapeDtypeStruct((M, N), out_dtype),
        compiler_params=_params("parallel", "parallel"),
        name="mm",
    )(a, b)


def _rms(x, g):
    return x * lax.rsqrt(jnp.mean(x * x, axis=-1, keepdims=True) + EPS) * g


def _rmsnorm_kernel(x_ref, g_ref, o_ref):
    o_ref[...] = _rms(x_ref[...], g_ref[...]).astype(o_ref.dtype)


def rmsnorm(x, g, out_dtype=BF16, tm=256):
    M, D = x.shape
    tm = min(tm, M)
    return pl.pallas_call(
        _rmsnorm_kernel,
        grid=(M // tm,),
        in_specs=[pl.BlockSpec((tm, D), lambda i: (i, 0)),
                  pl.BlockSpec((1, D), lambda i: (0, 0))],
        out_specs=pl.BlockSpec((tm, D), lambda i: (i, 0)),
        out_shape=jax.ShapeDtypeStruct((M, D), out_dtype),
        compiler_params=_params("parallel"),
        name="rmsnorm",
    )(x, g.reshape(1, D))


def _resid_norm_kernel(x_ref, y_ref, gp_ref, gn_ref, xo_ref, h_ref, *, coef):
    xn = x_ref[...] + coef * _rms(y_ref[...], gp_ref[...])
    xo_ref[...] = xn
    h_ref[...] = _rms(xn, gn_ref[...]).astype(h_ref.dtype)


def resid_norm(x, y, g_post, g_next, coef, tm=256):
    M, D = x.shape
    tm = min(tm, M)
    row = pl.BlockSpec((tm, D), lambda i: (i, 0))
    vec = pl.BlockSpec((1, D), lambda i: (0, 0))
    return pl.pallas_call(
        functools.partial(_resid_norm_kernel, coef=coef),
        grid=(M // tm,),
        in_specs=[row, row, vec, vec],
        out_specs=[row, row],
        out_shape=[jax.ShapeDtypeStruct((M, D), F32), jax.ShapeDtypeStruct((M, D), BF16)],
        compiler_params=_params("parallel"),
        name="resid_norm",
    )(x, y, g_post.reshape(1, D), g_next.reshape(1, D))


def _ffn_kernel(h_ref, w1_ref, w3_ref, w2_ref, o_ref):
    h = h_ref[...]
    a = _dot(h, w1_ref[...])
    b = _dot(h, w3_ref[...])
    u = (a * _sigmoid(a) * b).astype(BF16)
    part = _dot(u, w2_ref[...])

    @pl.when(pl.program_id(1) == 0)
    def _():
        o_ref[...] = part

    @pl.when(pl.program_id(1) != 0)
    def _():
        o_ref[...] += part


def ffn(h, w1, w3, w2, tm=512, tf=512):
    M, D = h.shape
    Fd = w1.shape[1]
    tm = min(tm, M)
    tf = min(tf, Fd)
    return pl.pallas_call(
        _ffn_kernel,
        grid=(M // tm, Fd // tf),
        in_specs=[pl.BlockSpec((tm, D), lambda i, j: (i, 0)),
                  pl.BlockSpec((D, tf), lambda i, j: (0, j)),
                  pl.BlockSpec((D, tf), lambda i, j: (0, j)),
                  pl.BlockSpec((tf, D), lambda i, j: (j, 0))],
        out_specs=pl.BlockSpec((tm, D), lambda i, j: (i, 0)),
        out_shape=jax.ShapeDtypeStruct((M, D), F32),
        compiler_params=_params("parallel", "arbitrary"),
        name="ffn",
    )(h, w1, w3, w2)


def _merge_kernel(ha_ref, ysb_ref, ygl_ref, yrw_ref, gc_ref, gb_ref, bw_ref, wo_ref, o_ref, *, rank):
    ha = ha_ref[...]
    m = None
    for g, y_ref in enumerate((ysb_ref, ygl_ref, yrw_ref)):
        gate = _sigmoid(_dot(ha[:, g * rank:(g + 1) * rank], gc_ref[g]) + gb_ref[g])
        term = gate * _dot(y_ref[...], bw_ref[g])
        m = term if m is None else m + term
    part = _dot(m.astype(BF16), wo_ref[...])

    @pl.when(pl.program_id(1) == 0)
    def _():
        o_ref[...] = part

    @pl.when(pl.program_id(1) != 0)
    def _():
        o_ref[...] += part


def merge(ha, y_sb, y_gla, y_rw, gate_c, gate_b, branch_w, w_out, tm=512, tn=512):
    M = ha.shape[0]
    nb, rank, D = gate_c.shape
    W = branch_w.shape[1]
    tm = min(tm, M)
    tn = min(tn, D)
    ybs = pl.BlockSpec((tm, W), lambda i, j: (i, 0))
    return pl.pallas_call(
        functools.partial(_merge_kernel, rank=rank),
        grid=(M // tm, D // tn),
        in_specs=[pl.BlockSpec((tm, nb * rank), lambda i, j: (i, 0)), ybs, ybs, ybs,
                  pl.BlockSpec((nb, rank, tn), lambda i, j: (0, 0, j)),
                  pl.BlockSpec((nb, 1, tn), lambda i, j: (0, 0, j)),
                  pl.BlockSpec((nb, W, tn), lambda i, j: (0, 0, j)),
                  pl.BlockSpec((tn, D), lambda i, j: (j, 0))],
        out_specs=pl.BlockSpec((tm, D), lambda i, j: (i, 0)),
        out_shape=jax.ShapeDtypeStruct((M, D), F32),
        compiler_params=_params("parallel", "arbitrary"),
        name="merge",
    )(ha, y_sb, y_gla, y_rw, gate_c, gate_b.reshape(nb, 1, D), branch_w, w_out)


def _xattn_kernel(h_ref, x_ref, wq_ref, k_ref, v_ref, wo_ref, gp_ref, gn_ref, xo_ref, ho_ref):
    q = _dot(h_ref[...], wq_ref[...])
    outs = []
    for hh in range(XA_HEADS):
        sl = slice(hh * XA_DH, (hh + 1) * XA_DH)
        s = _dot_nt(q[:, sl].astype(BF16), k_ref[:, sl]) * (XA_DH ** -0.5)
        e = jnp.exp(s - jnp.max(s, axis=-1, keepdims=True))
        p = e / jnp.sum(e, axis=-1, keepdims=True)
        outs.append(_dot(p.astype(BF16), v_ref[:, sl]).astype(BF16))
    y = _dot(jnp.concatenate(outs, axis=1), wo_ref[...])
    xn = x_ref[...] + _rms(y, gp_ref[...])
    xo_ref[...] = xn
    ho_ref[...] = _rms(xn, gn_ref[...]).astype(ho_ref.dtype)


def xattn_resid(h, x, wq, k, v, wo, g_post, g_next, tm=256):
    M, D = h.shape
    tm = min(tm, M)
    row = pl.BlockSpec((tm, D), lambda i: (i, 0))
    whole = lambda arr: pl.BlockSpec(arr.shape, lambda i: (0, 0), pipeline_mode=pl.Buffered(1))
    gp, gn = g_post.reshape(1, D), g_next.reshape(1, D)
    return pl.pallas_call(
        _xattn_kernel,
        grid=(M // tm,),
        in_specs=[row, row, whole(wq), whole(k), whole(v), whole(wo), whole(gp), whole(gn)],
        out_specs=[row, row],
        out_shape=[jax.ShapeDtypeStruct((M, D), F32), jax.ShapeDtypeStruct((M, D), BF16)],
        compiler_params=_params("parallel"),
        name="xattn_resid",
    )(h, x, wq, k, v, wo, gp, gn)


SB_SUB = 256
SB_ROWS = 256
SB_QSCALE = SB_DH ** -0.5 * math.log2(math.e)


def _sb_kernel(qi_tab, kj_tab, q_ref, k_ref, v_ref, o_ref, acc_ref, carry_ref, *, tb):
    step = pl.program_id(1)
    qi = qi_tab[step]
    kj = kj_tab[step]
    sub = min(SB_SUB, tb)
    nsub = tb // sub

    @pl.when(kj == qi)
    def _():
        acc_ref[...] = jnp.zeros_like(acc_ref)
        carry_ref[...] = jnp.zeros_like(carry_ref)

    def body(diag):
        jj = lax.broadcasted_iota(jnp.int32, (sub, sub), 0)
        ss = lax.broadcasted_iota(jnp.int32, (sub, sub), 1)
        neg_tri = jnp.where(jj > ss, -1.0, 0.0).astype(BF16)
        rb_rows = min(SB_ROWS, tb)

        def row_block(rb):
            r0 = rb * rb_rows
            rs = slice(r0, r0 + rb_rows)
            z_rb = _dot_nt(q_ref[rs, :], k_ref[...])
            yield
            carry = carry_ref[rs, :]
            a_parts = [None] * nsub
            for c in reversed(range(nsub)):
                c0 = c * sub
                if diag and r0 + rb_rows - 1 <= c0:
                    a_parts[c] = jnp.zeros((rb_rows, sub), BF16)
                    continue
                masked = diag and r0 <= c0 + sub - 1
                z = z_rb[:, c0:c0 + sub]
                sp = jnp.maximum(z, 0.0) + jnp.log2(1.0 + jnp.exp2(-jnp.abs(z)))
                if masked:
                    row = lax.broadcasted_iota(jnp.int32, z.shape, 0) + r0
                    col = lax.broadcasted_iota(jnp.int32, z.shape, 1) + c0
                    causal = col < row
                    sp = jnp.where(causal, sp, 0.0)
                spb = sp.astype(BF16)
                yield
                later = _dot(spb, neg_tri) + carry
                yield
                a = jnp.exp2(z - sp + later)
                if masked:
                    a = jnp.where(causal, a, 0.0)
                a_parts[c] = a.astype(BF16)
                carry = later[:, 0:1] - sp[:, 0:1]
                yield
            carry_ref[rs, :] = carry
            acc_ref[rs, :] += _dot(jnp.concatenate(a_parts, axis=1), v_ref[...])

        nrb = tb // rb_rows
        gens = [row_block(rb) for rb in range(nrb)]
        live = []
        pending = list(range(nrb))
        while pending or live:
            if pending:
                live.append(gens[pending.pop(0)])
            for g in list(live):
                try:
                    next(g)
                except StopIteration:
                    live.remove(g)

    @pl.when(kj == qi)
    def _():
        body(True)

    @pl.when(kj != qi)
    def _():
        body(False)

    @pl.when(kj == 0)
    def _():
        o_ref[...] = acc_ref[...].astype(o_ref.dtype)


def sb_attention(p_sb, tb=1024):
    T = p_sb.shape[0]
    tb = min(tb, T)
    nq = T // tb
    qi_l, kj_l = [], []
    for i in range(nq):
        for j in range(i, -1, -1):
            qi_l.append(i)
            kj_l.append(j)
    qi_tab = jnp.asarray(qi_l, jnp.int32)
    kj_tab = jnp.asarray(kj_l, jnp.int32)
    H = SB_HEADS
    grid_spec = pltpu.PrefetchScalarGridSpec(
        num_scalar_prefetch=2,
        grid=(H, len(qi_l)),
        in_specs=[pl.BlockSpec((tb, SB_DH), lambda h, s, qt, kt: (qt[s], h)),
                  pl.BlockSpec((tb, SB_DH), lambda h, s, qt, kt: (kt[s], H + h)),
                  pl.BlockSpec((tb, SB_DH), lambda h, s, qt, kt: (kt[s], 2 * H + h))],
        out_specs=pl.BlockSpec((tb, SB_DH), lambda h, s, qt, kt: (qt[s], h)),
        scratch_shapes=[pltpu.VMEM((tb, SB_DH), F32), pltpu.VMEM((tb, 1), F32)],
    )
    return pl.pallas_call(
        functools.partial(_sb_kernel, tb=tb),
        grid_spec=grid_spec,
        out_shape=jax.ShapeDtypeStruct((T, SB_W), BF16),
        compiler_params=_params("parallel", "arbitrary"),
        name="sb_attention",
    )(qi_tab, kj_tab, p_sb, p_sb, p_sb)


def _gla_kernel(q_ref, k_ref, v_ref, g_ref, ap_ref, a2_ref, ab_ref, ng_ref, o_ref, s_ref, *, tb):
    C = CHUNK

    @pl.when(pl.program_id(1) == 0)
    def _():
        s_ref[...] = jnp.zeros_like(s_ref)

    tt = lax.broadcasted_iota(jnp.int32, (C, C), 0)
    ss = lax.broadcasted_iota(jnp.int32, (C, C), 1)
    causal = ss <= tt
    tri = jnp.where(causal, 1.0, 0.0).astype(BF16)
    st = s_ref[...]
    for c in range(tb // C):
        sl = slice(c * C, (c + 1) * C)
        pre = _dot(ap_ref[sl, :].astype(BF16), a2_ref[...]) + ab_ref[...]
        la = -_softplus(-pre) * (1.0 / GLA_TAU)
        hi, lo = _split(la)
        b = _dot(tri, hi) + _dot(tri, lo)
        b_last = b[C - 1:C, :]
        k = k_ref[sl, :]
        q_in = (q_ref[sl, :] * (GLA_DK ** -0.5) * jnp.exp(b)).astype(BF16)
        k_in = (k * jnp.exp(-b)).astype(BF16)
        k_end = (k * jnp.exp(b_last - b)).astype(BF16)
        v = v_ref[sl, :].astype(BF16)
        scores = jnp.where(causal, _dot_nt(q_in, k_in), 0.0)
        o = _dot(scores.astype(BF16), v) + _dot_nt(q_in, st.astype(BF16))
        st = jnp.exp(b_last) * st + _dot_tn(v, k_end)
        o = o * lax.rsqrt(jnp.mean(o * o, axis=-1, keepdims=True) + EPS) * ng_ref[...]
        g = g_ref[sl, :]
        o_ref[sl, :] = (o * (g * _sigmoid(g))).astype(o_ref.dtype)
    s_ref[...] = st


def gla(p_gla, p_ap, ap_blk, a2p, ab, norm_g, tb=512):
    T = p_gla.shape[0]
    tb = min(tb, T)
    H = GLA_HEADS
    kq = GLA_KW // GLA_DK
    kv = 2 * GLA_KW // GLA_DV
    return pl.pallas_call(
        functools.partial(_gla_kernel, tb=tb),
        grid=(H, T // tb),
        in_specs=[pl.BlockSpec((tb, GLA_DK), lambda h, i: (i, h)),
                  pl.BlockSpec((tb, GLA_DK), lambda h, i: (i, kq + h)),
                  pl.BlockSpec((tb, GLA_DV), lambda h, i: (i, kv + h)),
                  pl.BlockSpec((tb, GLA_DV), lambda h, i: (i, kv + H + h)),
                  pl.BlockSpec((tb, LANES), lambda h, i: (i, ap_blk)),
                  pl.BlockSpec((LANES, GLA_DK), lambda h, i: (0, h)),
                  pl.BlockSpec((1, GLA_DK), lambda h, i: (0, h)),
                  pl.BlockSpec((1, GLA_DV), lambda h, i: (0, 0))],
        out_specs=pl.BlockSpec((tb, GLA_DV), lambda h, i: (i, h)),
        out_shape=jax.ShapeDtypeStruct((T, GLA_VW), BF16),
        scratch_shapes=[pltpu.VMEM((GLA_DV, GLA_DK), F32)],
        compiler_params=_params("parallel", "arbitrary"),
        name="gla",
    )(p_gla, p_gla, p_gla, p_gla, p_ap, a2p, ab.reshape(1, GLA_KW), norm_g.reshape(1, GLA_DV))


def _wkv_kernel(rd_ref, ad_ref, bh_ref, kh_ref, be_ref, ke_ref, v_ref, dec_ref, o_ref, s_ref, *, R, NS):
    C = RW_CHUNK
    N = RW_DH
    HB = RW_HEADS
    nch = R // C

    @pl.when(pl.program_id(0) == 0)
    def _():
        s_ref[...] = jnp.zeros_like(s_ref)

    row = lax.broadcasted_iota(jnp.int32, (R, R), 0)
    col = lax.broadcasted_iota(jnp.int32, (R, R), 1)
    same = (row // C) == (col // C)
    m_incl = same & (col <= row)
    m_strict = same & (col < row)
    m_incl_t = same & (row <= col)
    one = lambda m: jnp.where(m, 1.0, 0.0).astype(BF16)
    eye_r = one(row == col)
    eye_rf = jnp.where(row == col, 1.0, 0.0)
    ri = lax.broadcasted_iota(jnp.int32, (N, N), 0)
    ci = lax.broadcasted_iota(jnp.int32, (N, N), 1)
    eye_n = one(ri == ci)
    lane_chunk = lax.broadcasted_iota(jnp.int32, (N, R), 1) // C
    steps = C.bit_length() - 2

    def prepare(ti, hd):
        rows = slice(ti * R, (ti + 1) * R)
        lanes = slice(hd * N, (hd + 1) * N)
        r_d, a_d, b_h, k_h, b_e, k_e, v = (ref[rows, lanes] for ref in
                                           (rd_ref, ad_ref, bh_ref, kh_ref, be_ref, ke_ref, v_ref))
        gram = _dot_nt(jnp.concatenate([a_d, r_d], axis=0), jnp.concatenate([b_h, k_h], axis=0))
        l_ab = jnp.where(m_strict, gram[:R, :R], 0.0)
        l_ak = jnp.where(m_strict, gram[:R, R:], 0.0).astype(BF16)
        m_rk = jnp.where(m_incl, gram[R:, R:], 0.0).astype(BF16)
        m_rb_t = jnp.where(m_incl_t, _dot_nt(b_h, r_d), 0.0).astype(BF16)
        xt = _dot_nt(eye_n, jnp.concatenate([v, a_d, r_d], axis=0))
        v_t = xt[:, :R].astype(BF16)
        a_dt = xt[:, R:2 * R].astype(BF16)
        r_dt = xt[:, 2 * R:].astype(BF16)
        yield

        inv_m = eye_rf + l_ab
        p = l_ab
        lv_t = _dot_nt(v_t, l_ak).astype(BF16)
        o_loct = _dot_nt(v_t, m_rk)
        for _ in range(steps):
            pb = p.astype(BF16)
            p = _dot(pb, pb)
            yield
            inv_m = inv_m + _dot(inv_m.astype(BF16), p.astype(BF16))
            yield
        inv_b = inv_m.astype(BF16)
        w_at = _dot_nt(a_dt, inv_b).astype(BF16)
        u_vt = _dot_nt(lv_t, inv_b)
        yield
        stack = lambda x: jnp.concatenate(
            [jnp.where(lane_chunk == c, x, jnp.zeros_like(x)) for c in range(nch)], axis=0)
        p_c = _dot(stack(w_at), b_e).astype(BF16)
        b_c = _dot(jnp.concatenate([stack(u_vt.astype(BF16)), stack(v_t)], axis=1),
                   jnp.concatenate([b_e, k_e], axis=0))
        return dict(rhs=jnp.concatenate([w_at, r_dt], axis=1), u_vt=u_vt, p_c=p_c, b_c=b_c,
                    m_rb_t=m_rb_t, o_loct=o_loct)

    pre = {}
    st = [s_ref[hd] for hd in range(HB)]

    def scan(ti):
        entering = [[] for _ in range(HB)]
        for c in range(nch):
            for hd in range(HB):
                pr = pre[ti, hd]
                sb = st[hd].astype(BF16)
                entering[hd].append(sb)
                dec = dec_ref[ti * R + c * C:ti * R + c * C + 1, hd * N:(hd + 1) * N]
                st[hd] = dec * st[hd] + _dot(sb, pr['p_c'][c * N:(c + 1) * N, :]) + pr['b_c'][c * N:(c + 1) * N, :]
            yield
        for hd in range(HB):
            pr = pre[ti, hd]
            res = _dot(jnp.concatenate(entering[hd], axis=0), pr['rhs'])
            u_t = pr['u_vt']
            o_t = pr['o_loct']
            for c in range(nch):
                m = lane_chunk == c
                u_t = u_t + jnp.where(m, res[c * N:(c + 1) * N, :R], 0.0)
                o_t = o_t + jnp.where(m, res[c * N:(c + 1) * N, R:], 0.0)
            o_t = o_t + _dot(u_t.astype(BF16), pr['m_rb_t'])
            hi, lo = _split(o_t)
            o_ref[ti * R:(ti + 1) * R, hd * N:(hd + 1) * N] = _dot_nt(eye_r, hi) + _dot_nt(eye_r, lo)
            if hd % 4 == 3:
                yield

    def run(gens):
        while gens:
            for key in list(gens):
                try:
                    next(gens[key])
                except StopIteration as done:
                    pre[key] = done.value
                    del gens[key]

    run({(0, hd): prepare(0, hd) for hd in range(HB)})
    for ti in range(NS):
        gens = {('scan', ti): scan(ti)}
        if ti + 1 < NS:
            gens.update({(ti + 1, hd): prepare(ti + 1, hd) for hd in range(HB)})
        run(gens)
    for hd in range(HB):
        s_ref[hd] = st[hd]


def wkv7(rd, ad, bh, kh, be, ke, vb, dec, R=128, NS=2):
    T = rd.shape[0]
    R = min(R, T)
    NS = min(NS, T // R)
    spec = pl.BlockSpec((NS * R, RW_W), lambda i: (i, 0))
    return pl.pallas_call(
        functools.partial(_wkv_kernel, R=R, NS=NS),
        grid=(T // (NS * R),),
        in_specs=[spec] * 8,
        out_specs=spec,
        out_shape=jax.ShapeDtypeStruct((T, RW_W), F32),
        scratch_shapes=[pltpu.VMEM((RW_HEADS, RW_DH, RW_DH), F32)],
        compiler_params=_params("arbitrary"),
        name="wkv7",
    )(rd, ad, bh, kh, be, ke, vb, dec)


RW_LOW = (128, 128, 512, 128)
RW_VEC_ROWS = 16
HALO = 8


def _head_sum(x, bd):
    m = x.shape[0]
    hi = x.astype(BF16)
    r1 = x - hi.astype(F32)
    mid = r1.astype(BF16)
    lo = (r1 - mid.astype(F32)).astype(BF16)
    pieces = jnp.concatenate([hi, mid, lo], axis=0)
    outs = []
    for j in range(RW_W // LANES):
        s = _dot(pieces[:, j * LANES:(j + 1) * LANES], bd)
        outs.append(s[:m] + s[m:2 * m] + s[2 * m:])
    return jnp.concatenate(outs, axis=1)


def _head_ones():
    ri = lax.broadcasted_iota(jnp.int32, (LANES, LANES), 0)
    ci = lax.broadcasted_iota(jnp.int32, (LANES, LANES), 1)
    return jnp.where((ri // RW_DH) == (ci // RW_DH), 1.0, 0.0).astype(BF16)


def _rwkv_prep_kernel(*refs, tm, first):
    if first:
        (p_ref, halo_ref, vec_ref, w2_ref, a2_ref, g2_ref,
         rd_ref, ad_ref, bh_ref, kh_ref, be_ref, ke_ref, vb_ref, dec_ref, g_ref, bonus_ref, vf_out_ref) = refs
    else:
        (p_ref, halo_ref, vf_ref, vec_ref, w2_ref, a2_ref, g2_ref, v2_ref,
         rd_ref, ad_ref, bh_ref, kh_ref, be_ref, ke_ref, vb_ref, dec_ref, g_ref, bonus_ref) = refs
    C = RW_CHUNK
    W = RW_W
    slots = RW_LOW[:3] if first else RW_LOW
    nlr = sum(slots)
    has_prev = (pl.program_id(0) > 0).astype(F32)
    row1 = lax.broadcasted_iota(jnp.int32, (tm, 1), 0)

    def shifted(c0, c1):
        prev_last = halo_ref[HALO - 1:HALO, c0:c1] * has_prev
        return jnp.where(row1 == 0, prev_last, pltpu.roll(p_ref[:, c0:c1], 1, axis=0))

    mu_r, mu_k, mu_v, w0, a0, k_k, k_a, r_k, v0 = (vec_ref[i:i + 1, :] for i in range(9))
    r_p, k_p, v_p = (p_ref[:, i * W:(i + 1) * W] for i in range(3))
    r = r_p + (shifted(0, W) - r_p) * mu_r
    k = k_p + (shifted(W, 2 * W) - k_p) * mu_k
    v = v_p + (shifted(2 * W, 3 * W) - v_p) * mu_v
    low = p_ref[:, 3 * W:3 * W + nlr] + shifted(3 * W + nlr, 3 * W + 2 * nlr)
    o_w, o_a, o_g, o_v = 0, slots[0], slots[0] + slots[1], slots[0] + slots[1] + slots[2]
    w_log = -_softplus(-(w0 + _dot(jnp.tanh(low[:, o_w:o_a]).astype(BF16), w2_ref[...]))) - 0.5
    lw = -jnp.exp(w_log)
    a = _sigmoid(a0 + _dot(low[:, o_a:o_g].astype(BF16), a2_ref[...]))
    g_ref[...] = _dot(_sigmoid(low[:, o_g:o_v]).astype(BF16), g2_ref[...])
    if first:
        vf_out_ref[...] = v
    else:
        v = v + (vf_ref[...] - v) * _sigmoid(v0 + _dot(low[:, o_v:nlr].astype(BF16), v2_ref[...]))

    bd = _head_ones()
    kk = k * k_k
    kk = kk * lax.rsqrt(jnp.maximum(_head_sum(kk * kk, bd), 1e-24))
    k = k * (1.0 + (a - 1.0) * k_a)
    bonus_ref[...] = _head_sum(r * k * r_k, bd) * v
    b_vec = kk * a

    rr = lax.broadcasted_iota(jnp.int32, (tm, tm), 0)
    cc = lax.broadcasted_iota(jnp.int32, (tm, tm), 1)
    same = (rr // C) == (cc // C)
    cum_incl = jnp.where(same & (cc <= rr), 1.0, 0.0).astype(BF16)
    cum_all = jnp.where(same, 1.0, 0.0).astype(BF16)
    lw_hi, lw_lo = _split(lw)
    g_in = _dot(cum_incl, lw_hi) + _dot(cum_incl, lw_lo)
    g_last = _dot(cum_all, lw_hi) + _dot(cum_all, lw_lo)
    rd_ref[...] = (r * jnp.exp(g_in)).astype(BF16)
    ad_ref[...] = (-kk * jnp.exp(g_in - lw)).astype(BF16)
    inv = jnp.exp(-g_in)
    bh_ref[...] = (b_vec * inv).astype(BF16)
    kh_ref[...] = (k * inv).astype(BF16)
    to_end = jnp.exp(g_last - g_in)
    be_ref[...] = (b_vec * to_end).astype(BF16)
    ke_ref[...] = (k * to_end).astype(BF16)
    vb_ref[...] = v.astype(BF16)
    dec_ref[...] = jnp.exp(g_last)


def rwkv_prep(p_rw, v_first, vecs, w2, a2, g2, v2, tm=256):
    T, PW = p_rw.shape
    tm = min(tm, T)
    first = v_first is None
    row = lambda n: pl.BlockSpec((tm, n), lambda i: (i, 0))
    full = lambda arr: pl.BlockSpec(arr.shape, lambda i: (0, 0))
    halo = pl.BlockSpec((HALO, PW), lambda i: (jnp.maximum(i * (tm // HALO) - 1, 0), 0))
    ins = [p_rw, p_rw] + ([] if first else [v_first]) + [vecs, w2, a2, g2] + ([] if first else [v2])
    in_specs = ([row(PW), halo] + ([] if first else [row(RW_W)]) + [full(vecs), full(w2), full(a2), full(g2)]
                + ([] if first else [full(v2)]))
    n_bf, n_f32 = 7, (4 if first else 3)
    outs = pl.pallas_call(
        functools.partial(_rwkv_prep_kernel, tm=tm, first=first),
        grid=(T // tm,),
        in_specs=in_specs,
        out_specs=[row(RW_W)] * (n_bf + n_f32),
        out_shape=[jax.ShapeDtypeStruct((T, RW_W), BF16)] * n_bf + [jax.ShapeDtypeStruct((T, RW_W), F32)] * n_f32,
        compiler_params=_params("parallel"),
        name="rwkv_prep",
    )(*ins)
    return outs


def _rwkv_post_kernel(o_ref, g_ref, bonus_ref, lng_ref, lnb_ref, y_ref):
    bd = _head_ones()
    o = o_ref[...]
    d = o - _head_sum(o, bd) * (1.0 / RW_DH)
    var = _head_sum(d * d, bd) * (1.0 / RW_DH)
    y = d * lax.rsqrt(var + RW_LN_EPS) * lng_ref[...] + lnb_ref[...] + bonus_ref[...]
    y_ref[...] = (y * g_ref[...]).astype(y_ref.dtype)


def rwkv_post(o, g, bonus, ln_g, ln_b, tm=256):
    T = o.shape[0]
    tm = min(tm, T)
    row = pl.BlockSpec((tm, RW_W), lambda i: (i, 0))
    vec = pl.BlockSpec((1, RW_W), lambda i: (0, 0))
    return pl.pallas_call(
        _rwkv_post_kernel,
        grid=(T // tm,),
        in_specs=[row, row, row, vec, vec],
        out_specs=row,
        out_shape=jax.ShapeDtypeStruct((T, RW_W), BF16),
        compiler_params=_params("parallel"),
        name="rwkv_post",
    )(o, g, bonus, ln_g.reshape(1, RW_W), ln_b.reshape(1, RW_W))


def rwkv_time_mix(p_rw, v_first, vecs, w2, a2, g2, v2, ln_g, ln_b):
    outs = rwkv_prep(p_rw, v_first, vecs, w2, a2, g2, v2)
    rd, ad, bh, kh, be, ke, vb, dec, g, bonus = outs[:10]
    if v_first is None:
        v_first = outs[10]
    o = wkv7(rd, ad, bh, kh, be, ke, vb, dec)
    return rwkv_post(o, g, bonus, ln_g, ln_b), v_first


def _pad_cols(w, mult):
    n = w.shape[1]
    pad = (-n) % mult
    return w if pad == 0 else jnp.pad(w, ((0, 0), (0, pad)))


def kernel(x, mem, ffn1_pre, ffn1_post, ffn1_w1, ffn1_w3, ffn1_w2, mix_pre, mix_post, w_in, gla_a1, gla_a2, gla_ab, gla_norm, rw_mu_rkv, rw_mu_wag, rw_w0, rw_w1, rw_w2, rw_a0, rw_a1, rw_a2, rw_g1, rw_g2, rw_k_k, rw_k_a, rw_r_k, rw_ln_g, rw_ln_b, rw_mu_vl, rw_v0, rw_v1, rw_v2, branch_w, gate_a, gate_c, gate_b, w_out, xa_pre, xa_post, mem_norm, xa_wq, xa_wk, xa_wv, xa_wo, ffn2_pre, ffn2_post, ffn2_w1, ffn2_w3, ffn2_w2):
    B, T, D = x.shape
    depth = ffn1_pre.shape[0]
    bf = lambda t: t.astype(BF16)
    sb_end = 3 * SB_W
    gla_end = sb_end + 2 * GLA_KW + 2 * GLA_VW
    outs = []
    for bi in range(B):
        xs = x[bi]
        mem_b = mem[bi]
        h = rmsnorm(xs, ffn1_pre[0])
        v_first = None
        for l in range(depth):
            y = ffn(h, bf(ffn1_w1[l]), bf(ffn1_w3[l]), bf(ffn1_w2[l]))
            xs, h = resid_norm(xs, y, ffn1_post[l], mix_pre[l], 0.5)

            w_sb = jnp.concatenate([w_in[l][:, :SB_W] * SB_QSCALE, w_in[l][:, SB_W:sb_end]], axis=1)
            p_sb = mm(h, bf(w_sb), BF16)
            p_gla = mm(h, bf(w_in[l][:, sb_end:gla_end]))
            lows = [(rw_w1[l], rw_mu_wag[l, 0]), (rw_a1[l], rw_mu_wag[l, 1]), (rw_g1[l], rw_mu_wag[l, 2])]
            if l > 0:
                lows.append((rw_v1[l - 1], rw_mu_vl[l - 1]))
            slot = lambda wl, i: jnp.pad(wl, ((0, 0), (0, RW_LOW[i] - wl.shape[1])))
            w_rw = jnp.concatenate([w_in[l][:, gla_end:]]
                                   + [slot(wl * (1.0 - m)[:, None], i) for i, (wl, m) in enumerate(lows)]
                                   + [slot(wl * m[:, None], i) for i, (wl, m) in enumerate(lows)], axis=1)
            gla_ap_col = w_rw.shape[1]
            w_rw = jnp.concatenate([w_rw, _pad_cols(gla_a1[l], LANES)], axis=1)
            p_rw = mm(h, bf(_pad_cols(w_rw, 512)))
            ha = mm(h, bf(jnp.concatenate([gate_a[l, g] for g in range(gate_a.shape[1])], axis=1)), BF16)

            y_sb = sb_attention(p_sb)
            a2p = jnp.pad(gla_a2[l], ((0, LANES - gla_a2.shape[1]), (0, 0)))
            y_gla = gla(p_gla, p_rw, gla_ap_col // LANES, bf(a2p), gla_ab[l], gla_norm[l])
            v0 = rw_v0[l - 1] if l > 0 else jnp.zeros((RW_W,), F32)
            vec_rows = [rw_mu_rkv[l, 0], rw_mu_rkv[l, 1], rw_mu_rkv[l, 2], rw_w0[l], rw_a0[l], rw_k_k[l], rw_k_a[l],
                        rw_r_k[l].reshape(RW_W), v0]
            vecs = jnp.pad(jnp.stack(vec_rows, axis=0), ((0, RW_VEC_ROWS - len(vec_rows)), (0, 0)))
            pad_rows = lambda w2d, i: jnp.pad(w2d, ((0, RW_LOW[i] - w2d.shape[0]), (0, 0)))
            v2 = bf(pad_rows(rw_v2[l - 1], 3)) if l > 0 else None
            y_rw, v_first = rwkv_time_mix(p_rw, v_first, vecs, bf(pad_rows(rw_w2[l], 0)), bf(pad_rows(rw_a2[l], 1)),
                                          bf(pad_rows(rw_g2[l], 2)), v2, rw_ln_g[l], rw_ln_b[l])

            y = merge(ha, y_sb, y_gla, y_rw, bf(gate_c[l]), gate_b[l], bf(branch_w[l]), bf(w_out[l]))
            xs, h = resid_norm(xs, y, mix_post[l], xa_pre[l], 1.0)

            mem_n = rmsnorm(mem_b, mem_norm[l])
            kx = mm(mem_n, bf(xa_wk[l]), BF16)
            vx = mm(mem_n, bf(xa_wv[l]), BF16)
            xs, h = xattn_resid(h, xs, bf(xa_wq[l]), kx, vx, bf(xa_wo[l]), xa_post[l], ffn2_pre[l])

            y = ffn(h, bf(ffn2_w1[l]), bf(ffn2_w3[l]), bf(ffn2_w2[l]))
            g_next = ffn1_pre[l + 1] if l + 1 < depth else ffn1_pre[0]
            xs, h = resid_norm(xs, y, ffn2_post[l], g_next, 0.5)
        outs.append(xs)
    return jnp.stack(outs, axis=0)
```

```python
import functools
import math

import jax
import jax.numpy as jnp
from jax import lax
from jax.experimental import pallas as pl
from jax.experimental.pallas import tpu as pltpu

F32 = jnp.float32
BF16 = jnp.bfloat16

EPS = 1e-6
CHUNK = 64
SB_HEADS, SB_DH, SB_W = 4, 256, 1024
GLA_HEADS, GLA_DK, GLA_DV, GLA_KW, GLA_VW = 4, 128, 256, 512, 1024
GLA_TAU = 16.0
RW_HEADS, RW_DH, RW_W = 16, 64, 1024
RW_LN_EPS = 64e-5
RW_CHUNK = 16
XA_HEADS, XA_DH, XA_W = 4, 256, 1024
LANES = 128
VMEM_LIMIT = 56 * 1024 * 1024


def _params(*sem):
    return pltpu.CompilerParams(dimension_semantics=sem, vmem_limit_bytes=VMEM_LIMIT)


def _dot(a, b):
    return jnp.dot(a, b, preferred_element_type=F32)


def _dot_nt(a, b):
    return lax.dot_general(a, b, (((1,), (1,)), ((), ())), preferred_element_type=F32)


def _dot_tn(a, b):
    return lax.dot_general(a, b, (((0,), (0,)), ((), ())), preferred_element_type=F32)


def _split(x):
    hi = x.astype(BF16)
    lo = (x - hi.astype(F32)).astype(BF16)
    return hi, lo


def _softplus(z):
    return jnp.maximum(z, 0.0) + jnp.log1p(jnp.exp(-jnp.abs(z)))


def _sigmoid(z):
    return 1.0 / (1.0 + jnp.exp(-z))


def _tile(n, pref):
    t = min(n, pref)
    while n % t:
        t -= LANES
    return t


def _mm_kernel(a_ref, b_ref, o_ref):
    o_ref[...] = _dot(a_ref[...].astype(BF16), b_ref[...]).astype(o_ref.dtype)


def mm(a, b, out_dtype=F32, tm=1024, tn=512):
    M, K = a.shape
    N = b.shape[1]
    tm = min(tm, M)
    tn = _tile(N, tn)
    assert M % tm == 0 and N % tn == 0
    return pl.pallas_call(
        _mm_kernel,
        grid=(M // tm, N // tn),
        in_specs=[pl.BlockSpec((tm, K), lambda i, j: (i, 0)),
                  pl.BlockSpec((K, tn), lambda i, j: (0, j))],
        out_specs=pl.BlockSpec((tm, tn), lambda i, j: (i, j)),
        out_shape=jax.ShapeDtypeStruct((M, N), out_dtype),
        compiler_params=_params("parallel", "parallel"),
        name="mm",
    )(a, b)


def _rms(x, g):
    return x * lax.rsqrt(jnp.mean(x * x, axis=-1, keepdims=True) + EPS) * g


def _rmsnorm_kernel(x_ref, g_ref, o_ref):
    o_ref[...] = _rms(x_ref[...], g_ref[...]).astype(o_ref.dtype)


def rmsnorm(x, g, out_dtype=BF16, tm=256):
    M, D = x.shape
    tm = min(tm, M)
    return pl.pallas_call(
        _rmsnorm_kernel,
        grid=(M // tm,),
        in_specs=[pl.BlockSpec((tm, D), lambda i: (i, 0)),
                  pl.BlockSpec((1, D), lambda i: (0, 0))],
        out_specs=pl.BlockSpec((tm, D), lambda i: (i, 0)),
        out_shape=jax.ShapeDtypeStruct((M, D), out_dtype),
        compiler_params=_params("parallel"),
        name="rmsnorm",
    )(x, g.reshape(1, D))


def _resid_norm_kernel(x_ref, y_ref, gp_ref, gn_ref, xo_ref, h_ref, *, coef):
    xn = x_ref[...] + coef * _rms(y_ref[...], gp_ref[...])
    xo_ref[...] = xn
    h_ref[...] = _rms(xn, gn_ref[...]).astype(h_ref.dtype)


def resid_norm(x, y, g_post, g_next, coef, tm=256):
    M, D = x.shape
    tm = min(tm, M)
    row = pl.BlockSpec((tm, D), lambda i: (i, 0))
    vec = pl.BlockSpec((1, D), lambda i: (0, 0))
    return pl.pallas_call(
        functools.partial(_resid_norm_kernel, coef=coef),
        grid=(M // tm,),
        in_specs=[row, row, vec, vec],
        out_specs=[row, row],
        out_shape=[jax.ShapeDtypeStruct((M, D), F32), jax.ShapeDtypeStruct((M, D), BF16)],
        compiler_params=_params("parallel"),
        name="resid_norm",
    )(x, y, g_post.reshape(1, D), g_next.reshape(1, D))


def _ffn_kernel(h_ref, w1_ref, w3_ref, w2_ref, o_ref):
    h = h_ref[...]
    a = _dot(h, w1_ref[...])
    b = _dot(h, w3_ref[...])
    u = (a * _sigmoid(a) * b).astype(BF16)
    part = _dot(u, w2_ref[...])

    @pl.when(pl.program_id(1) == 0)
    def _():
        o_ref[...] = part

    @pl.when(pl.program_id(1) != 0)
    def _():
        o_ref[...] += part


def ffn(h, w1, w3, w2, tm=512, tf=512):
    M, D = h.shape
    Fd = w1.shape[1]
    tm = min(tm, M)
    tf = min(tf, Fd)
    return pl.pallas_call(
        _ffn_kernel,
        grid=(M // tm, Fd // tf),
        in_specs=[pl.BlockSpec((tm, D), lambda i, j: (i, 0)),
                  pl.BlockSpec((D, tf), lambda i, j: (0, j)),
                  pl.BlockSpec((D, tf), lambda i, j: (0, j)),
                  pl.BlockSpec((tf, D), lambda i, j: (j, 0))],
        out_specs=pl.BlockSpec((tm, D), lambda i, j: (i, 0)),
        out_shape=jax.ShapeDtypeStruct((M, D), F32),
        compiler_params=_params("parallel", "arbitrary"),
        name="ffn",
    )(h, w1, w3, w2)


def _merge_kernel(ha_ref, ysb_ref, ygl_ref, yrw_ref, gc_ref, gb_ref, bw_ref, wo_ref, o_ref, *, rank):
    ha = ha_ref[...]
    m = None
    for g, y_ref in enumerate((ysb_ref, ygl_ref, yrw_ref)):
        gate = _sigmoid(_dot(ha[:, g * rank:(g + 1) * rank], gc_ref[g]) + gb_ref[g])
        term = gate * _dot(y_ref[...], bw_ref[g])
        m = term if m is None else m + term
    part = _dot(m.astype(BF16), wo_ref[...])

    @pl.when(pl.program_id(1) == 0)
    def _():
        o_ref[...] = part

    @pl.when(pl.program_id(1) != 0)
    def _():
        o_ref[...] += part


def merge(ha, y_sb, y_gla, y_rw, gate_c, gate_b, branch_w, w_out, tm=512, tn=512):
    M = ha.shape[0]
    nb, rank, D = gate_c.shape
    W = branch_w.shape[1]
    tm = min(tm, M)
    tn = min(tn, D)
    ybs = pl.BlockSpec((tm, W), lambda i, j: (i, 0))
    return pl.pallas_call(
        functools.partial(_merge_kernel, rank=rank),
        grid=(M // tm, D // tn),
        in_specs=[pl.BlockSpec((tm, nb * rank), lambda i, j: (i, 0)), ybs, ybs, ybs,
                  pl.BlockSpec((nb, rank, tn), lambda i, j: (0, 0, j)),
                  pl.BlockSpec((nb, 1, tn), lambda i, j: (0, 0, j)),
                  pl.BlockSpec((nb, W, tn), lambda i, j: (0, 0, j)),
                  pl.BlockSpec((tn, D), lambda i, j: (j, 0))],
        out_specs=pl.BlockSpec((tm, D), lambda i, j: (i, 0)),
        out_shape=jax.ShapeDtypeStruct((M, D), F32),
        compiler_params=_params("parallel", "arbitrary"),
        name="merge",
    )(ha, y_sb, y_gla, y_rw, gate_c, gate_b.reshape(nb, 1, D), branch_w, w_out)


def _xattn_kernel(h_ref, x_ref, wq_ref, k_ref, v_ref, wo_ref, gp_ref, gn_ref, xo_ref, ho_ref):
    q = _dot(h_ref[...], wq_ref[...])
    outs = []
    for hh in range(XA_HEADS):
        sl = slice(hh * XA_DH, (hh + 1) * XA_DH)
        s = _dot_nt(q[:, sl].astype(BF16), k_ref[:, sl]) * (XA_DH ** -0.5)
        e = jnp.exp(s - jnp.max(s, axis=-1, keepdims=True))
        p = e / jnp.sum(e, axis=-1, keepdims=True)
        outs.append(_dot(p.astype(BF16), v_ref[:, sl]).astype(BF16))
    y = _dot(jnp.concatenate(outs, axis=1), wo_ref[...])
    xn = x_ref[...] + _rms(y, gp_ref[...])
    xo_ref[...] = xn
    ho_ref[...] = _rms(xn, gn_ref[...]).astype(ho_ref.dtype)


def xattn_resid(h, x, wq, k, v, wo, g_post, g_next, tm=256):
    M, D = h.shape
    tm = min(tm, M)
    row = pl.BlockSpec((tm, D), lambda i: (i, 0))
    whole = lambda arr: pl.BlockSpec(arr.shape, lambda i: (0, 0), pipeline_mode=pl.Buffered(1))
    gp, gn = g_post.reshape(1, D), g_next.reshape(1, D)
    return pl.pallas_call(
        _xattn_kernel,
        grid=(M // tm,),
        in_specs=[row, row, whole(wq), whole(k), whole(v), whole(wo), whole(gp), whole(gn)],
        out_specs=[row, row],
        out_shape=[jax.ShapeDtypeStruct((M, D), F32), jax.ShapeDtypeStruct((M, D), BF16)],
        compiler_params=_params("parallel"),
        name="xattn_resid",
    )(h, x, wq, k, v, wo, gp, gn)


SB_SUB = 256
SB_ROWS = 256
SB_QSCALE = SB_DH ** -0.5 * math.log2(math.e)


def _sb_kernel(qi_tab, kj_tab, q_ref, k_ref, v_ref, o_ref, acc_ref, carry_ref, *, tb):
    step = pl.program_id(1)
    qi = qi_tab[step]
    kj = kj_tab[step]
    sub = min(SB_SUB, tb)
    nsub = tb // sub

    @pl.when(kj == qi)
    def _():
        acc_ref[...] = jnp.zeros_like(acc_ref)
        carry_ref[...] = jnp.zeros_like(carry_ref)

    def body(diag):
        jj = lax.broadcasted_iota(jnp.int32, (sub, sub), 0)
        ss = lax.broadcasted_iota(jnp.int32, (sub, sub), 1)
        neg_tri = jnp.where(jj > ss, -1.0, 0.0).astype(BF16)
        rb_rows = min(SB_ROWS, tb)

        def row_block(rb):
            r0 = rb * rb_rows
            rs = slice(r0, r0 + rb_rows)
            z_rb = _dot_nt(q_ref[rs, :], k_ref[...])
            yield
            carry = carry_ref[rs, :]
            a_parts = [None] * nsub
            for c in reversed(range(nsub)):
                c0 = c * sub
                if diag and r0 + rb_rows - 1 <= c0:
                    a_parts[c] = jnp.zeros((rb_rows, sub), BF16)
                    continue
                masked = diag and r0 <= c0 + sub - 1
                z = z_rb[:, c0:c0 + sub]
                neg_abs = lax.bitcast_convert_type(
                    lax.bitcast_convert_type(z, jnp.uint32) | jnp.uint32(0x80000000), F32)
                sp = jnp.maximum(z, 0.0) + jnp.log2(1.0 + jnp.exp2(neg_abs))
                if masked:
                    row = lax.broadcasted_iota(jnp.int32, z.shape, 0) + r0
                    col = lax.broadcasted_iota(jnp.int32, z.shape, 1) + c0
                    causal = col < row
                    sp = jnp.where(causal, sp, 0.0)
                spb = sp.astype(BF16)
                yield
                later = _dot(spb, neg_tri) + carry
                yield
                a = jnp.exp2(z - sp + later)
                if masked:
                    a = jnp.where(causal, a, 0.0)
                a_parts[c] = a.astype(BF16)
                carry = later[:, 0:1] - sp[:, 0:1]
                yield
            carry_ref[rs, :] = carry
            acc_ref[rs, :] += _dot(jnp.concatenate(a_parts, axis=1), v_ref[...])

        nrb = tb // rb_rows
        gens = [row_block(rb) for rb in range(nrb)]
        live = []
        pending = list(range(nrb))
        while pending or live:
            if pending:
                live.append(gens[pending.pop(0)])
            for g in list(live):
                try:
                    next(g)
                except StopIteration:
                    live.remove(g)

    @pl.when(kj == qi)
    def _():
        body(True)

    @pl.when(kj != qi)
    def _():
        body(False)

    @pl.when(kj == 0)
    def _():
        o_ref[...] = acc_ref[...].astype(o_ref.dtype)


def sb_attention(p_sb, tb=1024):
    T = p_sb.shape[0]
    tb = min(tb, T)
    nq = T // tb
    qi_l, kj_l = [], []
    for i in range(nq):
        for j in range(i, -1, -1):
            qi_l.append(i)
            kj_l.append(j)
    qi_tab = jnp.asarray(qi_l, jnp.int32)
    kj_tab = jnp.asarray(kj_l, jnp.int32)
    H = SB_HEADS
    grid_spec = pltpu.PrefetchScalarGridSpec(
        num_scalar_prefetch=2,
        grid=(H, len(qi_l)),
        in_specs=[pl.BlockSpec((tb, SB_DH), lambda h, s, qt, kt: (qt[s], h)),
                  pl.BlockSpec((tb, SB_DH), lambda h, s, qt, kt: (kt[s], H + h)),
                  pl.BlockSpec((tb, SB_DH), lambda h, s, qt, kt: (kt[s], 2 * H + h))],
        out_specs=pl.BlockSpec((tb, SB_DH), lambda h, s, qt, kt: (qt[s], h)),
        scratch_shapes=[pltpu.VMEM((tb, SB_DH), F32), pltpu.VMEM((tb, 1), F32)],
    )
    return pl.pallas_call(
        functools.partial(_sb_kernel, tb=tb),
        grid_spec=grid_spec,
        out_shape=jax.ShapeDtypeStruct((T, SB_W), BF16),
        compiler_params=_params("parallel", "arbitrary"),
        name="sb_attention",
    )(qi_tab, kj_tab, p_sb, p_sb, p_sb)


def _gla_kernel(q_ref, k_ref, v_ref, g_ref, ap_ref, a2_ref, ab_ref, ng_ref, o_ref, s_ref, *, tb):
    C = CHUNK
    nch = tb // C

    @pl.when(pl.program_id(0) == 0)
    def _():
        s_ref[...] = jnp.zeros_like(s_ref)

    rr = lax.broadcasted_iota(jnp.int32, (tb, tb), 0)
    cc = lax.broadcasted_iota(jnp.int32, (tb, tb), 1)
    same = (rr // C) == (cc // C)
    cum_incl = jnp.where(same & (cc <= rr), 1.0, 0.0).astype(BF16)
    cum_all = jnp.where(same, 1.0, 0.0).astype(BF16)
    pre = _dot(ap_ref[...].astype(BF16), a2_ref[...]) + ab_ref[...]
    la = -_softplus(-pre) * (1.0 / GLA_TAU)
    hi, lo = _split(la)
    b = _dot(cum_incl, hi) + _dot(cum_incl, lo)
    b_last = _dot(cum_all, hi) + _dot(cum_all, lo)
    k = k_ref[...]
    q_in = (q_ref[...] * (GLA_DK ** -0.5) * jnp.exp(b)).astype(BF16)
    k_in = (k * jnp.exp(-b)).astype(BF16)
    k_end = (k * jnp.exp(b_last - b)).astype(BF16)
    dec = jnp.exp(b_last)

    tt = lax.broadcasted_iota(jnp.int32, (C, C), 0)
    ss = lax.broadcasted_iota(jnp.int32, (C, C), 1)
    causal = ss <= tt

    def head(hh):
        kc = slice(hh * GLA_DK, (hh + 1) * GLA_DK)
        vc = slice(hh * GLA_DV, (hh + 1) * GLA_DV)
        st = s_ref[hh]
        for c in range(nch):
            sl = slice(c * C, (c + 1) * C)
            qi = q_in[sl, kc]
            v = v_ref[sl, vc].astype(BF16)
            scores = jnp.where(causal, _dot_nt(qi, k_in[sl, kc]), 0.0)
            kv = _dot_tn(v, k_end[sl, kc])
            yield
            o = _dot(scores.astype(BF16), v) + _dot_nt(qi, st.astype(BF16))
            st = dec[c * C:c * C + 1, kc] * st + kv
            yield
            o = o * lax.rsqrt(jnp.mean(o * o, axis=-1, keepdims=True) + EPS) * ng_ref[...]
            g = g_ref[sl, vc]
            o_ref[sl, vc] = (o * (g * _sigmoid(g))).astype(o_ref.dtype)
        s_ref[hh] = st

    live = [head(hh) for hh in range(GLA_HEADS)]
    while live:
        for gen in list(live):
            try:
                next(gen)
            except StopIteration:
                live.remove(gen)


def gla(p_gla, p_ap, ap_blk, a2p, ab, norm_g, tb=512):
    T = p_gla.shape[0]
    tb = min(tb, T)
    assert 2 * GLA_KW == GLA_VW
    return pl.pallas_call(
        functools.partial(_gla_kernel, tb=tb),
        grid=(T // tb,),
        in_specs=[pl.BlockSpec((tb, GLA_KW), lambda i: (i, 0)),
                  pl.BlockSpec((tb, GLA_KW), lambda i: (i, 1)),
                  pl.BlockSpec((tb, GLA_VW), lambda i: (i, 1)),
                  pl.BlockSpec((tb, GLA_VW), lambda i: (i, 2)),
                  pl.BlockSpec((tb, LANES), lambda i: (i, ap_blk)),
                  pl.BlockSpec((LANES, GLA_KW), lambda i: (0, 0)),
                  pl.BlockSpec((1, GLA_KW), lambda i: (0, 0)),
                  pl.BlockSpec((1, GLA_DV), lambda i: (0, 0))],
        out_specs=pl.BlockSpec((tb, GLA_VW), lambda i: (i, 0)),
        out_shape=jax.ShapeDtypeStruct((T, GLA_VW), BF16),
        scratch_shapes=[pltpu.VMEM((GLA_HEADS, GLA_DV, GLA_DK), F32)],
        compiler_params=_params("arbitrary"),
        name="gla",
    )(p_gla, p_gla, p_gla, p_gla, p_ap, a2p, ab.reshape(1, GLA_KW), norm_g.reshape(1, GLA_DV))


def _wkv_kernel(rd_ref, ad_ref, bh_ref, kh_ref, be_ref, ke_ref, v_ref, dec_ref, o_ref,
                s_ref, rhs_s, uvt_s, pc_s, bc_s, mrb_s, oloc_s, *, R, NS):
    C = RW_CHUNK
    N = RW_DH
    HB = RW_HEADS
    nch = R // C
    j = pl.program_id(0)

    @pl.when(j == 0)
    def _():
        for ref in (s_ref, rhs_s, uvt_s, pc_s, bc_s, mrb_s, oloc_s):
            ref[...] = jnp.zeros_like(ref)

    w_base = (j % 2) * (NS * HB)
    r_base = (1 - j % 2) * (NS * HB)

    row = lax.broadcasted_iota(jnp.int32, (R, R), 0)
    col = lax.broadcasted_iota(jnp.int32, (R, R), 1)
    same = (row // C) == (col // C)
    m_incl = same & (col <= row)
    m_strict = same & (col < row)
    m_incl_t = same & (row <= col)
    one = lambda m: jnp.where(m, 1.0, 0.0).astype(BF16)
    eye_r = one(row == col)
    eye_rf = jnp.where(row == col, 1.0, 0.0)
    ri = lax.broadcasted_iota(jnp.int32, (N, N), 0)
    ci = lax.broadcasted_iota(jnp.int32, (N, N), 1)
    eye_n = one(ri == ci)
    lane_chunk = lax.broadcasted_iota(jnp.int32, (N, R), 1) // C
    steps = C.bit_length() - 2

    def prepare(ti, hd):
        rows = slice(ti * R, (ti + 1) * R)
        lanes = slice(hd * N, (hd + 1) * N)
        r_d, a_d, b_h, k_h, b_e, k_e, v = (ref[rows, lanes] for ref in
                                           (rd_ref, ad_ref, bh_ref, kh_ref, be_ref, ke_ref, v_ref))
        gram = _dot_nt(jnp.concatenate([a_d, r_d], axis=0), jnp.concatenate([b_h, k_h], axis=0))
        l_ab = jnp.where(m_strict, gram[:R, :R], 0.0)
        l_ak = jnp.where(m_strict, gram[:R, R:], 0.0).astype(BF16)
        m_rk = jnp.where(m_incl, gram[R:, R:], 0.0).astype(BF16)
        m_rb_t = jnp.where(m_incl_t, _dot_nt(b_h, r_d), 0.0).astype(BF16)
        xt = _dot_nt(eye_n, jnp.concatenate([v, a_d, r_d], axis=0))
        v_t = xt[:, :R].astype(BF16)
        a_dt = xt[:, R:2 * R].astype(BF16)
        r_dt = xt[:, 2 * R:].astype(BF16)
        yield

        inv_m = eye_rf + l_ab
        p = l_ab
        lv_t = _dot_nt(v_t, l_ak).astype(BF16)
        o_loct = _dot_nt(v_t, m_rk)
        for _ in range(steps):
            pb = p.astype(BF16)
            p = _dot(pb, pb)
            yield
            inv_m = inv_m + _dot(inv_m.astype(BF16), p.astype(BF16))
            yield
        inv_b = inv_m.astype(BF16)
        w_at = _dot_nt(a_dt, inv_b).astype(BF16)
        u_vt = _dot_nt(lv_t, inv_b)
        yield
        stack = lambda x: jnp.concatenate(
            [jnp.where(lane_chunk == c, x, jnp.zeros_like(x)) for c in range(nch)], axis=0)
        p_c = _dot(stack(w_at), b_e).astype(BF16)
        b_c = _dot(jnp.concatenate([stack(u_vt.astype(BF16)), stack(v_t)], axis=1),
                   jnp.concatenate([b_e, k_e], axis=0))
        iw = w_base + ti * HB + hd
        rhs_s[iw] = jnp.concatenate([w_at, r_dt], axis=1)
        uvt_s[iw] = u_vt
        pc_s[iw] = p_c
        bc_s[iw] = b_c
        mrb_s[iw] = m_rb_t
        oloc_s[iw] = o_loct

    st = [s_ref[hd] for hd in range(HB)]

    def scan(ti):
        entering = [[] for _ in range(HB)]
        for c in range(nch):
            for hd in range(HB):
                ir = r_base + ti * HB + hd
                sb = st[hd].astype(BF16)
                entering[hd].append(sb)
                dec = dec_ref[ti * R + c * C:ti * R + c * C + 1, hd * N:(hd + 1) * N]
                st[hd] = dec * st[hd] + _dot(sb, pc_s[ir, c * N:(c + 1) * N, :]) + bc_s[ir, c * N:(c + 1) * N, :]
            yield

        def outputs(hd):
            ir = r_base + ti * HB + hd
            res = _dot(jnp.concatenate(entering[hd], axis=0), rhs_s[ir])
            yield
            u_t = uvt_s[ir]
            o_t = oloc_s[ir]
            for c in range(nch):
                m = lane_chunk == c
                u_t = u_t + jnp.where(m, res[c * N:(c + 1) * N, :R], 0.0)
                o_t = o_t + jnp.where(m, res[c * N:(c + 1) * N, R:], 0.0)
            o_t = o_t + _dot(u_t.astype(BF16), mrb_s[ir])
            yield
            hi, lo = _split(o_t)
            o_ref[ti * R:(ti + 1) * R, hd * N:(hd + 1) * N] = _dot_nt(eye_r, hi) + _dot_nt(eye_r, lo)

        outs = [outputs(hd) for hd in range(HB)]
        while outs:
            for gen in list(outs):
                try:
                    next(gen)
                except StopIteration:
                    outs.remove(gen)
            yield

    def scans():
        for ti in range(NS):
            yield from scan(ti)

    live = [scans()] + [prepare(ti, hd) for ti in range(NS) for hd in range(HB)]
    while live:
        for gen in list(live):
            try:
                next(gen)
            except StopIteration:
                live.remove(gen)
    for hd in range(HB):
        s_ref[hd] = st[hd]


def wkv7(rd, ad, bh, kh, be, ke, vb, dec, R=128, NS=1):
    T = rd.shape[0]
    R = min(R, T)
    NS = min(NS, T // R)
    nblk = T // (NS * R)
    N, H = RW_DH, RW_HEADS
    nbuf = 2 * NS * H
    prep_spec = pl.BlockSpec((NS * R, RW_W), lambda j: (jnp.minimum(j, nblk - 1), 0))
    scan_spec = pl.BlockSpec((NS * R, RW_W), lambda j: (jnp.maximum(j - 1, 0), 0))
    return pl.pallas_call(
        functools.partial(_wkv_kernel, R=R, NS=NS),
        grid=(nblk + 1,),
        in_specs=[prep_spec] * 7 + [scan_spec],
        out_specs=scan_spec,
        out_shape=jax.ShapeDtypeStruct((T, RW_W), F32),
        scratch_shapes=[pltpu.VMEM((H, N, N), F32),
                        pltpu.VMEM((nbuf, N, 2 * R), BF16), pltpu.VMEM((nbuf, N, R), F32),
                        pltpu.VMEM((nbuf, (R // RW_CHUNK) * N, N), BF16),
                        pltpu.VMEM((nbuf, (R // RW_CHUNK) * N, N), F32),
                        pltpu.VMEM((nbuf, R, R), BF16), pltpu.VMEM((nbuf, N, R), F32)],
        compiler_params=_params("arbitrary"),
        name="wkv7",
    )(rd, ad, bh, kh, be, ke, vb, dec)


RW_LOW = (128, 128, 512, 128)
RW_VEC_ROWS = 16
HALO = 8


def _head_sum(x, bd):
    m = x.shape[0]
    hi = x.astype(BF16)
    r1 = x - hi.astype(F32)
    mid = r1.astype(BF16)
    lo = (r1 - mid.astype(F32)).astype(BF16)
    pieces = jnp.concatenate([hi, mid, lo], axis=0)
    outs = []
    for j in range(RW_W // LANES):
        s = _dot(pieces[:, j * LANES:(j + 1) * LANES], bd)
        outs.append(s[:m] + s[m:2 * m] + s[2 * m:])
    return jnp.concatenate(outs, axis=1)


def _head_ones():
    ri = lax.broadcasted_iota(jnp.int32, (LANES, LANES), 0)
    ci = lax.broadcasted_iota(jnp.int32, (LANES, LANES), 1)
    return jnp.where((ri // RW_DH) == (ci // RW_DH), 1.0, 0.0).astype(BF16)


def _rwkv_prep_kernel(*refs, tm, first):
    if first:
        (p_ref, halo_ref, vec_ref, w2_ref, a2_ref, g2_ref,
         rd_ref, ad_ref, bh_ref, kh_ref, be_ref, ke_ref, vb_ref, dec_ref, g_ref, bonus_ref, vf_out_ref) = refs
    else:
        (p_ref, halo_ref, vf_ref, vec_ref, w2_ref, a2_ref, g2_ref, v2_ref,
         rd_ref, ad_ref, bh_ref, kh_ref, be_ref, ke_ref, vb_ref, dec_ref, g_ref, bonus_ref) = refs
    C = RW_CHUNK
    W = RW_W
    slots = RW_LOW[:3] if first else RW_LOW
    nlr = sum(slots)
    has_prev = (pl.program_id(0) > 0).astype(F32)
    row1 = lax.broadcasted_iota(jnp.int32, (tm, 1), 0)

    def shifted(c0, c1):
        prev_last = halo_ref[HALO - 1:HALO, c0:c1] * has_prev
        return jnp.where(row1 == 0, prev_last, pltpu.roll(p_ref[:, c0:c1], 1, axis=0))

    mu_r, mu_k, mu_v, w0, a0, k_k, k_a, r_k, v0 = (vec_ref[i:i + 1, :] for i in range(9))
    r_p, k_p, v_p = (p_ref[:, i * W:(i + 1) * W] for i in range(3))
    r = r_p + (shifted(0, W) - r_p) * mu_r
    k = k_p + (shifted(W, 2 * W) - k_p) * mu_k
    v = v_p + (shifted(2 * W, 3 * W) - v_p) * mu_v
    low = p_ref[:, 3 * W:3 * W + nlr] + shifted(3 * W + nlr, 3 * W + 2 * nlr)
    o_w, o_a, o_g, o_v = 0, slots[0], slots[0] + slots[1], slots[0] + slots[1] + slots[2]
    w_log = -_softplus(-(w0 + _dot(jnp.tanh(low[:, o_w:o_a]).astype(BF16), w2_ref[...]))) - 0.5
    lw = -jnp.exp(w_log)
    a = _sigmoid(a0 + _dot(low[:, o_a:o_g].astype(BF16), a2_ref[...]))
    g_ref[...] = _dot(_sigmoid(low[:, o_g:o_v]).astype(BF16), g2_ref[...])
    if first:
        vf_out_ref[...] = v
    else:
        v = v + (vf_ref[...] - v) * _sigmoid(v0 + _dot(low[:, o_v:nlr].astype(BF16), v2_ref[...]))

    bd = _head_ones()
    kk = k * k_k
    kk = kk * lax.rsqrt(jnp.maximum(_head_sum(kk * kk, bd), 1e-24))
    k = k * (1.0 + (a - 1.0) * k_a)
    bonus_ref[...] = _head_sum(r * k * r_k, bd) * v
    b_vec = kk * a

    rr = lax.broadcasted_iota(jnp.int32, (tm, tm), 0)
    cc = lax.broadcasted_iota(jnp.int32, (tm, tm), 1)
    same = (rr // C) == (cc // C)
    cum_incl = jnp.where(same & (cc <= rr), 1.0, 0.0).astype(BF16)
    cum_all = jnp.where(same, 1.0, 0.0).astype(BF16)
    lw_hi, lw_lo = _split(lw)
    g_in = _dot(cum_incl, lw_hi) + _dot(cum_incl, lw_lo)
    g_last = _dot(cum_all, lw_hi) + _dot(cum_all, lw_lo)
    rd_ref[...] = (r * jnp.exp(g_in)).astype(BF16)
    ad_ref[...] = (-kk * jnp.exp(g_in - lw)).astype(BF16)
    inv = jnp.exp(-g_in)
    bh_ref[...] = (b_vec * inv).astype(BF16)
    kh_ref[...] = (k * inv).astype(BF16)
    to_end = jnp.exp(g_last - g_in)
    be_ref[...] = (b_vec * to_end).astype(BF16)
    ke_ref[...] = (k * to_end).astype(BF16)
    vb_ref[...] = v.astype(BF16)
    dec_ref[...] = jnp.exp(g_last)


def rwkv_prep(p_rw, v_first, vecs, w2, a2, g2, v2, tm=256):
    T, PW = p_rw.shape
    tm = min(tm, T)
    first = v_first is None
    row = lambda n: pl.BlockSpec((tm, n), lambda i: (i, 0))
    full = lambda arr: pl.BlockSpec(arr.shape, lambda i: (0, 0))
    halo = pl.BlockSpec((HALO, PW), lambda i: (jnp.maximum(i * (tm // HALO) - 1, 0), 0))
    ins = [p_rw, p_rw] + ([] if first else [v_first]) + [vecs, w2, a2, g2] + ([] if first else [v2])
    in_specs = ([row(PW), halo] + ([] if first else [row(RW_W)]) + [full(vecs), full(w2), full(a2), full(g2)]
                + ([] if first else [full(v2)]))
    n_bf, n_f32 = 7, (4 if first else 3)
    outs = pl.pallas_call(
        functools.partial(_rwkv_prep_kernel, tm=tm, first=first),
        grid=(T // tm,),
        in_specs=in_specs,
        out_specs=[row(RW_W)] * (n_bf + n_f32),
        out_shape=[jax.ShapeDtypeStruct((T, RW_W), BF16)] * n_bf + [jax.ShapeDtypeStruct((T, RW_W), F32)] * n_f32,
        compiler_params=_params("parallel"),
        name="rwkv_prep",
    )(*ins)
    return outs


def _rwkv_post_kernel(o_ref, g_ref, bonus_ref, lng_ref, lnb_ref, y_ref):
    bd = _head_ones()
    o = o_ref[...]
    d = o - _head_sum(o, bd) * (1.0 / RW_DH)
    var = _head_sum(d * d, bd) * (1.0 / RW_DH)
    y = d * lax.rsqrt(var + RW_LN_EPS) * lng_ref[...] + lnb_ref[...] + bonus_ref[...]
    y_ref[...] = (y * g_ref[...]).astype(y_ref.dtype)


def rwkv_post(o, g, bonus, ln_g, ln_b, tm=256):
    T = o.shape[0]
    tm = min(tm, T)
    row = pl.BlockSpec((tm, RW_W), lambda i: (i, 0))
    vec = pl.BlockSpec((1, RW_W), lambda i: (0, 0))
    return pl.pallas_call(
        _rwkv_post_kernel,
        grid=(T // tm,),
        in_specs=[row, row, row, vec, vec],
        out_specs=row,
        out_shape=jax.ShapeDtypeStruct((T, RW_W), BF16),
        compiler_params=_params("parallel"),
        name="rwkv_post",
    )(o, g, bonus, ln_g.reshape(1, RW_W), ln_b.reshape(1, RW_W))


def rwkv_time_mix(p_rw, v_first, vecs, w2, a2, g2, v2, ln_g, ln_b):
    outs = rwkv_prep(p_rw, v_first, vecs, w2, a2, g2, v2)
    rd, ad, bh, kh, be, ke, vb, dec, g, bonus = outs[:10]
    if v_first is None:
        v_first = outs[10]
    o = wkv7(rd, ad, bh, kh, be, ke, vb, dec)
    return rwkv_post(o, g, bonus, ln_g, ln_b), v_first


def _pad_cols(w, mult):
    n = w.shape[1]
    pad = (-n) % mult
    return w if pad == 0 else jnp.pad(w, ((0, 0), (0, pad)))


def kernel(x, mem, ffn1_pre, ffn1_post, ffn1_w1, ffn1_w3, ffn1_w2, mix_pre, mix_post, w_in, gla_a1, gla_a2, gla_ab, gla_norm, rw_mu_rkv, rw_mu_wag, rw_w0, rw_w1, rw_w2, rw_a0, rw_a1, rw_a2, rw_g1, rw_g2, rw_k_k, rw_k_a, rw_r_k, rw_ln_g, rw_ln_b, rw_mu_vl, rw_v0, rw_v1, rw_v2, branch_w, gate_a, gate_c, gate_b, w_out, xa_pre, xa_post, mem_norm, xa_wq, xa_wk, xa_wv, xa_wo, ffn2_pre, ffn2_post, ffn2_w1, ffn2_w3, ffn2_w2):
    B, T, D = x.shape
    depth = ffn1_pre.shape[0]
    bf = lambda t: t.astype(BF16)
    sb_end = 3 * SB_W
    gla_end = sb_end + 2 * GLA_KW + 2 * GLA_VW
    outs = []
    for bi in range(B):
        xs = x[bi]
        mem_b = mem[bi]
        h = rmsnorm(xs, ffn1_pre[0])
        v_first = None
        for l in range(depth):
            y = ffn(h, bf(ffn1_w1[l]), bf(ffn1_w3[l]), bf(ffn1_w2[l]))
            xs, h = resid_norm(xs, y, ffn1_post[l], mix_pre[l], 0.5)

            w_sb = jnp.concatenate([w_in[l][:, :SB_W] * SB_QSCALE, w_in[l][:, SB_W:sb_end]], axis=1)
            p_sb = mm(h, bf(w_sb), BF16)
            p_gla = mm(h, bf(w_in[l][:, sb_end:gla_end]))
            lows = [(rw_w1[l], rw_mu_wag[l, 0]), (rw_a1[l], rw_mu_wag[l, 1]), (rw_g1[l], rw_mu_wag[l, 2])]
            if l > 0:
                lows.append((rw_v1[l - 1], rw_mu_vl[l - 1]))
            slot = lambda wl, i: jnp.pad(wl, ((0, 0), (0, RW_LOW[i] - wl.shape[1])))
            w_rw = jnp.concatenate([w_in[l][:, gla_end:]]
                                   + [slot(wl * (1.0 - m)[:, None], i) for i, (wl, m) in enumerate(lows)]
                                   + [slot(wl * m[:, None], i) for i, (wl, m) in enumerate(lows)], axis=1)
            gla_ap_col = w_rw.shape[1]
            w_rw = jnp.concatenate([w_rw, _pad_cols(gla_a1[l], LANES)], axis=1)
            p_rw = mm(h, bf(_pad_cols(w_rw, 512)))
            ha = mm(h, bf(jnp.concatenate([gate_a[l, g] for g in range(gate_a.shape[1])], axis=1)), BF16)

            y_sb = sb_attention(p_sb)
            a2p = jnp.pad(gla_a2[l], ((0, LANES - gla_a2.shape[1]), (0, 0)))
            y_gla = gla(p_gla, p_rw, gla_ap_col // LANES, bf(a2p), gla_ab[l], gla_norm[l])
            v0 = rw_v0[l - 1] if l > 0 else jnp.zeros((RW_W,), F32)
            vec_rows = [rw_mu_rkv[l, 0], rw_mu_rkv[l, 1], rw_mu_rkv[l, 2], rw_w0[l], rw_a0[l], rw_k_k[l], rw_k_a[l],
                        rw_r_k[l].reshape(RW_W), v0]
            vecs = jnp.pad(jnp.stack(vec_rows, axis=0), ((0, RW_VEC_ROWS - len(vec_rows)), (0, 0)))
            pad_rows = lambda w2d, i: jnp.pad(w2d, ((0, RW_LOW[i] - w2d.shape[0]), (0, 0)))
            v2 = bf(pad_rows(rw_v2[l - 1], 3)) if l > 0 else None
            y_rw, v_first = rwkv_time_mix(p_rw, v_first, vecs, bf(pad_rows(rw_w2[l], 0)), bf(pad_rows(rw_a2[l], 1)),
                                          bf(pad_rows(rw_g2[l], 2)), v2, rw_ln_g[l], rw_ln_b[l])

            y = merge(ha, y_sb, y_gla, y_rw, bf(gate_c[l]), gate_b[l], bf(branch_w[l]), bf(w_out[l]))
            xs, h = resid_norm(xs, y, mix_post[l], xa_pre[l], 1.0)

            mem_n = rmsnorm(mem_b, mem_norm[l])
            kx = mm(mem_n, bf(xa_wk[l]), BF16)
            vx = mm(mem_n, bf(xa_wv[l]), BF16)
            xs, h = xattn_resid(h, xs, bf(xa_wq[l]), kx, vx, bf(xa_wo[l]), xa_post[l], ffn2_pre[l])

            y = ffn(h, bf(ffn2_w1[l]), bf(ffn2_w3[l]), bf(ffn2_w2[l]))
            g_next = ffn1_pre[l + 1] if l + 1 < depth else ffn1_pre[0]
            xs, h = resid_norm(xs, y, ffn2_post[l], g_next, 0.5)
        outs.append(xs)
    return jnp.stack(outs, axis=0)
```

```python
import functools
import math

import jax
import jax.numpy as jnp
from jax import lax
from jax.experimental import pallas as pl
from jax.experimental.pallas import tpu as pltpu

F32 = jnp.float32
BF16 = jnp.bfloat16

EPS = 1e-6
CHUNK = 64
SB_HEADS, SB_DH, SB_W = 4, 256, 1024
GLA_HEADS, GLA_DK, GLA_DV, GLA_KW, GLA_VW = 4, 128, 256, 512, 1024
GLA_TAU = 16.0
RW_HEADS, RW_DH, RW_W = 16, 64, 1024
RW_LN_EPS = 64e-5
RW_CHUNK = 16
XA_HEADS, XA_DH, XA_W = 4, 256, 1024
LANES = 128
VMEM_LIMIT = 56 * 1024 * 1024


def _params(*sem):
    return pltpu.CompilerParams(dimension_semantics=sem, vmem_limit_bytes=VMEM_LIMIT)


def _dot(a, b):
    return jnp.dot(a, b, preferred_element_type=F32)


def _dot_nt(a, b):
    return lax.dot_general(a, b, (((1,), (1,)), ((), ())), preferred_element_type=F32)


def _dot_tn(a, b):
    return lax.dot_general(a, b, (((0,), (0,)), ((), ())), preferred_element_type=F32)


def _split(x):
    hi = x.astype(BF16)
    lo = (x - hi.astype(F32)).astype(BF16)
    return hi, lo


def _softplus(z):
    return jnp.maximum(z, 0.0) + jnp.log1p(jnp.exp(-jnp.abs(z)))


def _sigmoid(z):
    return 1.0 / (1.0 + jnp.exp(-z))


def _tile(n, pref):
    t = min(n, pref)
    while n % t:
        t -= LANES
    return t


def _mm_kernel(a_ref, b_ref, o_ref):
    o_ref[...] = _dot(a_ref[...].astype(BF16), b_ref[...]).astype(o_ref.dtype)


def _mm_scaled_kernel(a_ref, b_ref, s_ref, o_ref):
    o_ref[...] = (_dot(a_ref[...].astype(BF16), b_ref[...]) * s_ref[...]).astype(o_ref.dtype)


def mm(a, b, out_dtype=F32, tm=1024, tn=1024, layer=None, col0=0, ncols=None, col_scale=None):
    M, K = a.shape
    N = b.shape[-1] - col0 if ncols is None else ncols
    tm = min(tm, M)
    tn = _tile(math.gcd(N, col0) if col0 else N, tn)
    assert M % tm == 0 and N % tn == 0 and col0 % tn == 0
    c0 = col0 // tn
    if layer is None:
        b_spec = pl.BlockSpec((K, tn), lambda i, j: (0, c0 + j))
    else:
        b_spec = pl.BlockSpec((None, K, tn), lambda i, j: (layer, 0, c0 + j))
    in_specs = [pl.BlockSpec((tm, K), lambda i, j: (i, 0)), b_spec]
    args = [a, b]
    if col_scale is not None:
        in_specs.append(pl.BlockSpec((1, tn), lambda i, j: (0, j)))
        args.append(col_scale.reshape(1, N).astype(F32))
    return pl.pallas_call(
        _mm_kernel if col_scale is None else _mm_scaled_kernel,
        grid=(M // tm, N // tn),
        in_specs=in_specs,
        out_specs=pl.BlockSpec((tm, tn), lambda i, j: (i, j)),
        out_shape=jax.ShapeDtypeStruct((M, N), out_dtype),
        compiler_params=_params("parallel", "parallel"),
        name="mm",
    )(*args)


def _rms(x, g):
    return x * lax.rsqrt(jnp.mean(x * x, axis=-1, keepdims=True) + EPS) * g


def _rmsnorm_kernel(x_ref, g_ref, o_ref):
    o_ref[...] = _rms(x_ref[...], g_ref[...]).astype(o_ref.dtype)


def rmsnorm(x, g, out_dtype=BF16, tm=256):
    M, D = x.shape
    tm = min(tm, M)
    return pl.pallas_call(
        _rmsnorm_kernel,
        grid=(M // tm,),
        in_specs=[pl.BlockSpec((tm, D), lambda i: (i, 0)),
                  pl.BlockSpec((1, D), lambda i: (0, 0))],
        out_specs=pl.BlockSpec((tm, D), lambda i: (i, 0)),
        out_shape=jax.ShapeDtypeStruct((M, D), out_dtype),
        compiler_params=_params("parallel"),
        name="rmsnorm",
    )(x, g.reshape(1, D))


def _resid_norm_kernel(x_ref, y_ref, gp_ref, gn_ref, xo_ref, h_ref, *, coef):
    xn = x_ref[...] + coef * _rms(y_ref[...], gp_ref[...])
    xo_ref[...] = xn
    h_ref[...] = _rms(xn, gn_ref[...]).astype(h_ref.dtype)


def resid_norm(x, y, g_post, g_next, coef, tm=256):
    M, D = x.shape
    tm = min(tm, M)
    row = pl.BlockSpec((tm, D), lambda i: (i, 0))
    vec = pl.BlockSpec((1, D), lambda i: (0, 0))
    return pl.pallas_call(
        functools.partial(_resid_norm_kernel, coef=coef),
        grid=(M // tm,),
        in_specs=[row, row, vec, vec],
        out_specs=[row, row],
        out_shape=[jax.ShapeDtypeStruct((M, D), F32), jax.ShapeDtypeStruct((M, D), BF16)],
        compiler_params=_params("parallel"),
        name="resid_norm",
    )(x, y, g_post.reshape(1, D), g_next.reshape(1, D))


def _ffn_kernel(h_ref, w1_ref, w3_ref, w2_ref, o_ref):
    h = h_ref[...]
    a = _dot(h, w1_ref[...])
    b = _dot(h, w3_ref[...])
    u = (a * _sigmoid(a) * b).astype(BF16)
    part = _dot(u, w2_ref[...])

    @pl.when(pl.program_id(1) == 0)
    def _():
        o_ref[...] = part

    @pl.when(pl.program_id(1) != 0)
    def _():
        o_ref[...] += part


def ffn(h, w1, w3, w2, tm=512, tf=512):
    M, D = h.shape
    Fd = w1.shape[1]
    tm = min(tm, M)
    tf = min(tf, Fd)
    return pl.pallas_call(
        _ffn_kernel,
        grid=(M // tm, Fd // tf),
        in_specs=[pl.BlockSpec((tm, D), lambda i, j: (i, 0)),
                  pl.BlockSpec((D, tf), lambda i, j: (0, j)),
                  pl.BlockSpec((D, tf), lambda i, j: (0, j)),
                  pl.BlockSpec((tf, D), lambda i, j: (j, 0))],
        out_specs=pl.BlockSpec((tm, D), lambda i, j: (i, 0)),
        out_shape=jax.ShapeDtypeStruct((M, D), F32),
        compiler_params=_params("parallel", "arbitrary"),
        name="ffn",
    )(h, w1, w3, w2)


def _merge_kernel(ha_ref, ysb_ref, ygl_ref, yrw_ref, gc_ref, gb_ref, bw_ref, wo_ref, o_ref, *, rank):
    ha = ha_ref[...]
    m = None
    for g, y_ref in enumerate((ysb_ref, ygl_ref, yrw_ref)):
        gate = _sigmoid(_dot(ha[:, g * rank:(g + 1) * rank], gc_ref[g]) + gb_ref[g])
        term = gate * _dot(y_ref[...], bw_ref[g])
        m = term if m is None else m + term
    part = _dot(m.astype(BF16), wo_ref[...])

    @pl.when(pl.program_id(1) == 0)
    def _():
        o_ref[...] = part

    @pl.when(pl.program_id(1) != 0)
    def _():
        o_ref[...] += part


def merge(ha, y_sb, y_gla, y_rw, gate_c, gate_b, branch_w, w_out, tm=512, tn=512):
    M = ha.shape[0]
    nb, rank, D = gate_c.shape
    W = branch_w.shape[1]
    tm = min(tm, M)
    tn = min(tn, D)
    ybs = pl.BlockSpec((tm, W), lambda i, j: (i, 0))
    return pl.pallas_call(
        functools.partial(_merge_kernel, rank=rank),
        grid=(M // tm, D // tn),
        in_specs=[pl.BlockSpec((tm, nb * rank), lambda i, j: (i, 0)), ybs, ybs, ybs,
                  pl.BlockSpec((nb, rank, tn), lambda i, j: (0, 0, j)),
                  pl.BlockSpec((nb, 1, tn), lambda i, j: (0, 0, j)),
                  pl.BlockSpec((nb, W, tn), lambda i, j: (0, 0, j)),
                  pl.BlockSpec((tn, D), lambda i, j: (j, 0))],
        out_specs=pl.BlockSpec((tm, D), lambda i, j: (i, 0)),
        out_shape=jax.ShapeDtypeStruct((M, D), F32),
        compiler_params=_params("parallel", "arbitrary"),
        name="merge",
    )(ha, y_sb, y_gla, y_rw, gate_c, gate_b.reshape(nb, 1, D), branch_w, w_out)


def _xattn_kernel(h_ref, x_ref, wq_ref, k_ref, v_ref, wo_ref, gp_ref, gn_ref, xo_ref, ho_ref):
    q = _dot(h_ref[...], wq_ref[...])
    outs = []
    for hh in range(XA_HEADS):
        sl = slice(hh * XA_DH, (hh + 1) * XA_DH)
        s = _dot_nt(q[:, sl].astype(BF16), k_ref[:, sl]) * (XA_DH ** -0.5)
        e = jnp.exp(s - jnp.max(s, axis=-1, keepdims=True))
        p = e / jnp.sum(e, axis=-1, keepdims=True)
        outs.append(_dot(p.astype(BF16), v_ref[:, sl]).astype(BF16))
    y = _dot(jnp.concatenate(outs, axis=1), wo_ref[...])
    xn = x_ref[...] + _rms(y, gp_ref[...])
    xo_ref[...] = xn
    ho_ref[...] = _rms(xn, gn_ref[...]).astype(ho_ref.dtype)


def xattn_resid(h, x, wq, k, v, wo, g_post, g_next, tm=256):
    M, D = h.shape
    tm = min(tm, M)
    row = pl.BlockSpec((tm, D), lambda i: (i, 0))
    whole = lambda arr: pl.BlockSpec(arr.shape, lambda i: (0, 0), pipeline_mode=pl.Buffered(1))
    gp, gn = g_post.reshape(1, D), g_next.reshape(1, D)
    return pl.pallas_call(
        _xattn_kernel,
        grid=(M // tm,),
        in_specs=[row, row, whole(wq), whole(k), whole(v), whole(wo), whole(gp), whole(gn)],
        out_specs=[row, row],
        out_shape=[jax.ShapeDtypeStruct((M, D), F32), jax.ShapeDtypeStruct((M, D), BF16)],
        compiler_params=_params("parallel"),
        name="xattn_resid",
    )(h, x, wq, k, v, wo, gp, gn)


SB_SUB = 256
SB_ROWS = 256
SB_QSCALE = SB_DH ** -0.5 * math.log2(math.e)


def _sb_kernel(qi_tab, kj_tab, q_ref, k_ref, v_ref, o_ref, acc_ref, carry_ref, *, tb):
    step = pl.program_id(1)
    qi = qi_tab[step]
    kj = kj_tab[step]
    sub = min(SB_SUB, tb)
    nsub = tb // sub

    @pl.when(kj == qi)
    def _():
        acc_ref[...] = jnp.zeros_like(acc_ref)
        carry_ref[...] = jnp.zeros_like(carry_ref)

    def body(diag):
        jj = lax.broadcasted_iota(jnp.int32, (sub, sub), 0)
        ss = lax.broadcasted_iota(jnp.int32, (sub, sub), 1)
        neg_tri = jnp.where(jj > ss, -1.0, 0.0).astype(BF16)
        rb_rows = min(SB_ROWS, tb)

        def row_block(rb):
            r0 = rb * rb_rows
            rs = slice(r0, r0 + rb_rows)
            z_rb = _dot_nt(q_ref[rs, :], k_ref[...])
            yield
            carry = carry_ref[rs, :]
            a_parts = [None] * nsub
            for c in reversed(range(nsub)):
                c0 = c * sub
                if diag and r0 + rb_rows - 1 <= c0:
                    a_parts[c] = jnp.zeros((rb_rows, sub), BF16)
                    continue
                masked = diag and r0 <= c0 + sub - 1
                z = z_rb[:, c0:c0 + sub]
                neg_abs = lax.bitcast_convert_type(
                    lax.bitcast_convert_type(z, jnp.uint32) | jnp.uint32(0x80000000), F32)
                sp = jnp.maximum(z, 0.0) + jnp.log2(1.0 + jnp.exp2(neg_abs))
                if masked:
                    row = lax.broadcasted_iota(jnp.int32, z.shape, 0) + r0
                    col = lax.broadcasted_iota(jnp.int32, z.shape, 1) + c0
                    causal = col < row
                    sp = jnp.where(causal, sp, 0.0)
                spb = sp.astype(BF16)
                yield
                later = _dot(spb, neg_tri) + carry
                yield
                a = jnp.exp2(z - sp + later)
                if masked:
                    a = jnp.where(causal, a, 0.0)
                a_parts[c] = a.astype(BF16)
                carry = later[:, 0:1] - sp[:, 0:1]
                yield
            carry_ref[rs, :] = carry
            acc_ref[rs, :] += _dot(jnp.concatenate(a_parts, axis=1), v_ref[...])

        nrb = tb // rb_rows
        gens = [row_block(rb) for rb in range(nrb)]
        live = []
        pending = list(range(nrb))
        while pending or live:
            if pending:
                live.append(gens[pending.pop(0)])
            for g in list(live):
                try:
                    next(g)
                except StopIteration:
                    live.remove(g)

    @pl.when(kj == qi)
    def _():
        body(True)

    @pl.when(kj != qi)
    def _():
        body(False)

    @pl.when(kj == 0)
    def _():
        o_ref[...] = acc_ref[...].astype(o_ref.dtype)


def sb_attention(p_sb, tb=2048):
    T = p_sb.shape[0]
    tb = min(tb, T)
    nq = T // tb
    qi_l, kj_l = [], []
    for i in range(nq):
        for j in range(i, -1, -1):
            qi_l.append(i)
            kj_l.append(j)
    qi_tab = jnp.asarray(qi_l, jnp.int32)
    kj_tab = jnp.asarray(kj_l, jnp.int32)
    H = SB_HEADS
    grid_spec = pltpu.PrefetchScalarGridSpec(
        num_scalar_prefetch=2,
        grid=(H, len(qi_l)),
        in_specs=[pl.BlockSpec((tb, SB_DH), lambda h, s, qt, kt: (qt[s], h)),
                  pl.BlockSpec((tb, SB_DH), lambda h, s, qt, kt: (kt[s], H + h)),
                  pl.BlockSpec((tb, SB_DH), lambda h, s, qt, kt: (kt[s], 2 * H + h))],
        out_specs=pl.BlockSpec((tb, SB_DH), lambda h, s, qt, kt: (qt[s], h)),
        scratch_shapes=[pltpu.VMEM((tb, SB_DH), F32), pltpu.VMEM((tb, 1), F32)],
    )
    return pl.pallas_call(
        functools.partial(_sb_kernel, tb=tb),
        grid_spec=grid_spec,
        out_shape=jax.ShapeDtypeStruct((T, SB_W), BF16),
        compiler_params=_params("parallel", "arbitrary"),
        name="sb_attention",
    )(qi_tab, kj_tab, p_sb, p_sb, p_sb)


def _gla_kernel(q_ref, k_ref, v_ref, g_ref, ap_ref, a2_ref, ab_ref, ng_ref, o_ref, s_ref, *, tb):
    C = CHUNK
    nch = tb // C

    @pl.when(pl.program_id(0) == 0)
    def _():
        s_ref[...] = jnp.zeros_like(s_ref)

    rr = lax.broadcasted_iota(jnp.int32, (tb, tb), 0)
    cc = lax.broadcasted_iota(jnp.int32, (tb, tb), 1)
    same = (rr // C) == (cc // C)
    cum_incl = jnp.where(same & (cc <= rr), 1.0, 0.0).astype(BF16)
    cum_all = jnp.where(same, 1.0, 0.0).astype(BF16)
    pre = _dot(ap_ref[...].astype(BF16), a2_ref[...]) + ab_ref[...]
    la = -_softplus(-pre) * (1.0 / GLA_TAU)
    hi, lo = _split(la)
    b = _dot(cum_incl, hi) + _dot(cum_incl, lo)
    b_last = _dot(cum_all, hi) + _dot(cum_all, lo)
    k = k_ref[...]
    q_in = (q_ref[...] * (GLA_DK ** -0.5) * jnp.exp(b)).astype(BF16)
    k_in = (k * jnp.exp(-b)).astype(BF16)
    k_end = (k * jnp.exp(b_last - b)).astype(BF16)
    dec = jnp.exp(b_last)

    tt = lax.broadcasted_iota(jnp.int32, (C, C), 0)
    ss = lax.broadcasted_iota(jnp.int32, (C, C), 1)
    causal = ss <= tt

    def head(hh):
        kc = slice(hh * GLA_DK, (hh + 1) * GLA_DK)
        vc = slice(hh * GLA_DV, (hh + 1) * GLA_DV)
        st = s_ref[hh]
        for c in range(nch):
            sl = slice(c * C, (c + 1) * C)
            qi = q_in[sl, kc]
            v = v_ref[sl, vc].astype(BF16)
            scores = jnp.where(causal, _dot_nt(qi, k_in[sl, kc]), 0.0)
            kv = _dot_tn(v, k_end[sl, kc])
            yield
            o = _dot(scores.astype(BF16), v) + _dot_nt(qi, st.astype(BF16))
            st = dec[c * C:c * C + 1, kc] * st + kv
            yield
            o = o * lax.rsqrt(jnp.mean(o * o, axis=-1, keepdims=True) + EPS) * ng_ref[...]
            g = g_ref[sl, vc]
            o_ref[sl, vc] = (o * (g * _sigmoid(g))).astype(o_ref.dtype)
        s_ref[hh] = st

    live = [head(hh) for hh in range(GLA_HEADS)]
    while live:
        for gen in list(live):
            try:
                next(gen)
            except StopIteration:
                live.remove(gen)


def gla(p_gla, p_ap, ap_blk, a2p, ab, norm_g, tb=512):
    T = p_gla.shape[0]
    tb = min(tb, T)
    assert 2 * GLA_KW == GLA_VW
    return pl.pallas_call(
        functools.partial(_gla_kernel, tb=tb),
        grid=(T // tb,),
        in_specs=[pl.BlockSpec((tb, GLA_KW), lambda i: (i, 0)),
                  pl.BlockSpec((tb, GLA_KW), lambda i: (i, 1)),
                  pl.BlockSpec((tb, GLA_VW), lambda i: (i, 1)),
                  pl.BlockSpec((tb, GLA_VW), lambda i: (i, 2)),
                  pl.BlockSpec((tb, LANES), lambda i: (i, ap_blk)),
                  pl.BlockSpec((LANES, GLA_KW), lambda i: (0, 0)),
                  pl.BlockSpec((1, GLA_KW), lambda i: (0, 0)),
                  pl.BlockSpec((1, GLA_DV), lambda i: (0, 0))],
        out_specs=pl.BlockSpec((tb, GLA_VW), lambda i: (i, 0)),
        out_shape=jax.ShapeDtypeStruct((T, GLA_VW), BF16),
        scratch_shapes=[pltpu.VMEM((GLA_HEADS, GLA_DV, GLA_DK), F32)],
        compiler_params=_params("arbitrary"),
        name="gla",
    )(p_gla, p_gla, p_gla, p_gla, p_ap, a2p, ab.reshape(1, GLA_KW), norm_g.reshape(1, GLA_DV))


def _wkv_kernel(rd_ref, ad_ref, bh_ref, kh_ref, be_ref, ke_ref, v_ref, dec_ref, o_ref,
                s_ref, rhs_s, uvt_s, pc_s, bc_s, mrb_s, oloc_s, *, R, NS):
    C = RW_CHUNK
    N = RW_DH
    HB = RW_HEADS
    nch = R // C
    j = pl.program_id(0)

    @pl.when(j == 0)
    def _():
        for ref in (s_ref, rhs_s, uvt_s, pc_s, bc_s, mrb_s, oloc_s):
            ref[...] = jnp.zeros_like(ref)

    w_base = (j % 2) * (NS * HB)
    r_base = (1 - j % 2) * (NS * HB)

    row = lax.broadcasted_iota(jnp.int32, (R, R), 0)
    col = lax.broadcasted_iota(jnp.int32, (R, R), 1)
    same = (row // C) == (col // C)
    m_incl = same & (col <= row)
    m_strict = same & (col < row)
    m_incl_t = same & (row <= col)
    one = lambda m: jnp.where(m, 1.0, 0.0).astype(BF16)
    eye_r = one(row == col)
    eye_rf = jnp.where(row == col, 1.0, 0.0)
    ri = lax.broadcasted_iota(jnp.int32, (N, N), 0)
    ci = lax.broadcasted_iota(jnp.int32, (N, N), 1)
    eye_n = one(ri == ci)
    lane_chunk = lax.broadcasted_iota(jnp.int32, (N, R), 1) // C
    steps = C.bit_length() - 2

    def prepare(ti, hd):
        rows = slice(ti * R, (ti + 1) * R)
        lanes = slice(hd * N, (hd + 1) * N)
        r_d, a_d, b_h, k_h, b_e, k_e, v = (ref[rows, lanes] for ref in
                                           (rd_ref, ad_ref, bh_ref, kh_ref, be_ref, ke_ref, v_ref))
        gram = _dot_nt(jnp.concatenate([a_d, r_d], axis=0), jnp.concatenate([b_h, k_h], axis=0))
        l_ab = jnp.where(m_strict, gram[:R, :R], 0.0)
        l_ak = jnp.where(m_strict, gram[:R, R:], 0.0).astype(BF16)
        m_rk = jnp.where(m_incl, gram[R:, R:], 0.0).astype(BF16)
        m_rb_t = jnp.where(m_incl_t, _dot_nt(b_h, r_d), 0.0).astype(BF16)
        xt = _dot_nt(eye_n, jnp.concatenate([v, a_d, r_d], axis=0))
        v_t = xt[:, :R].astype(BF16)
        a_dt = xt[:, R:2 * R].astype(BF16)
        r_dt = xt[:, 2 * R:].astype(BF16)
        yield

        inv_m = eye_rf + l_ab
        p = l_ab
        lv_t = _dot_nt(v_t, l_ak).astype(BF16)
        o_loct = _dot_nt(v_t, m_rk)
        for _ in range(steps):
            pb = p.astype(BF16)
            p = _dot(pb, pb)
            yield
            inv_m = inv_m + _dot(inv_m.astype(BF16), p.astype(BF16))
            yield
        inv_b = inv_m.astype(BF16)
        w_at = _dot_nt(a_dt, inv_b).astype(BF16)
        u_vt = _dot_nt(lv_t, inv_b)
        yield
        stack = lambda x: jnp.concatenate(
            [jnp.where(lane_chunk == c, x, jnp.zeros_like(x)) for c in range(nch)], axis=0)
        p_c = _dot(stack(w_at), b_e).astype(BF16)
        b_c = _dot(jnp.concatenate([stack(u_vt.astype(BF16)), stack(v_t)], axis=1),
                   jnp.concatenate([b_e, k_e], axis=0))
        iw = w_base + ti * HB + hd
        rhs_s[iw] = jnp.concatenate([w_at, r_dt], axis=1)
        uvt_s[iw] = u_vt
        pc_s[iw] = p_c
        bc_s[iw] = b_c
        mrb_s[iw] = m_rb_t
        oloc_s[iw] = o_loct

    st = [s_ref[hd] for hd in range(HB)]

    def scan(ti):
        entering = [[] for _ in range(HB)]
        for c in range(nch):
            for hd in range(HB):
                ir = r_base + ti * HB + hd
                sb = st[hd].astype(BF16)
                entering[hd].append(sb)
                dec = dec_ref[ti * R + c * C:ti * R + c * C + 1, hd * N:(hd + 1) * N]
                st[hd] = dec * st[hd] + _dot(sb, pc_s[ir, c * N:(c + 1) * N, :]) + bc_s[ir, c * N:(c + 1) * N, :]
            yield

        def outputs(hd):
            ir = r_base + ti * HB + hd
            res = _dot(jnp.concatenate(entering[hd], axis=0), rhs_s[ir])
            yield
            u_t = uvt_s[ir]
            o_t = oloc_s[ir]
            for c in range(nch):
                m = lane_chunk == c
                u_t = u_t + jnp.where(m, res[c * N:(c + 1) * N, :R], 0.0)
                o_t = o_t + jnp.where(m, res[c * N:(c + 1) * N, R:], 0.0)
            o_t = o_t + _dot(u_t.astype(BF16), mrb_s[ir])
            yield
            hi, lo = _split(o_t)
            o_ref[ti * R:(ti + 1) * R, hd * N:(hd + 1) * N] = _dot_nt(eye_r, hi) + _dot_nt(eye_r, lo)

        outs = [outputs(hd) for hd in range(HB)]
        while outs:
            for gen in list(outs):
                try:
                    next(gen)
                except StopIteration:
                    outs.remove(gen)
            yield

    def scans():
        for ti in range(NS):
            yield from scan(ti)

    live = [scans()] + [prepare(ti, hd) for ti in range(NS) for hd in range(HB)]
    while live:
        for gen in list(live):
            try:
                next(gen)
            except StopIteration:
                live.remove(gen)
    for hd in range(HB):
        s_ref[hd] = st[hd]


def wkv7(rd, ad, bh, kh, be, ke, vb, dec, R=128, NS=1):
    T = rd.shape[0]
    R = min(R, T)
    NS = min(NS, T // R)
    nblk = T // (NS * R)
    N, H = RW_DH, RW_HEADS
    nbuf = 2 * NS * H
    prep_spec = pl.BlockSpec((NS * R, RW_W), lambda j: (jnp.minimum(j, nblk - 1), 0))
    scan_spec = pl.BlockSpec((NS * R, RW_W), lambda j: (jnp.maximum(j - 1, 0), 0))
    return pl.pallas_call(
        functools.partial(_wkv_kernel, R=R, NS=NS),
        grid=(nblk + 1,),
        in_specs=[prep_spec] * 7 + [scan_spec],
        out_specs=scan_spec,
        out_shape=jax.ShapeDtypeStruct((T, RW_W), F32),
        scratch_shapes=[pltpu.VMEM((H, N, N), F32),
                        pltpu.VMEM((nbuf, N, 2 * R), BF16), pltpu.VMEM((nbuf, N, R), F32),
                        pltpu.VMEM((nbuf, (R // RW_CHUNK) * N, N), BF16),
                        pltpu.VMEM((nbuf, (R // RW_CHUNK) * N, N), F32),
                        pltpu.VMEM((nbuf, R, R), BF16), pltpu.VMEM((nbuf, N, R), F32)],
        compiler_params=_params("arbitrary"),
        name="wkv7",
    )(rd, ad, bh, kh, be, ke, vb, dec)


RW_LOW = (128, 128, 512, 128)
RW_VEC_ROWS = 16
HALO = 8


def _head_sum(x, bd):
    m = x.shape[0]
    hi = x.astype(BF16)
    r1 = x - hi.astype(F32)
    mid = r1.astype(BF16)
    lo = (r1 - mid.astype(F32)).astype(BF16)
    pieces = jnp.concatenate([hi, mid, lo], axis=0)
    outs = []
    for j in range(RW_W // LANES):
        s = _dot(pieces[:, j * LANES:(j + 1) * LANES], bd)
        outs.append(s[:m] + s[m:2 * m] + s[2 * m:])
    return jnp.concatenate(outs, axis=1)


def _head_ones():
    ri = lax.broadcasted_iota(jnp.int32, (LANES, LANES), 0)
    ci = lax.broadcasted_iota(jnp.int32, (LANES, LANES), 1)
    return jnp.where((ri // RW_DH) == (ci // RW_DH), 1.0, 0.0).astype(BF16)


def _rwkv_prep_kernel(*refs, tm, first):
    if first:
        (p_ref, halo_ref, pl_ref, halo_l_ref, vec_ref, w2_ref, a2_ref, g2_ref,
         rd_ref, ad_ref, bh_ref, kh_ref, be_ref, ke_ref, vb_ref, dec_ref, g_ref, bonus_ref, vf_out_ref) = refs
    else:
        (p_ref, halo_ref, pl_ref, halo_l_ref, vf_ref, vec_ref, w2_ref, a2_ref, g2_ref, v2_ref,
         rd_ref, ad_ref, bh_ref, kh_ref, be_ref, ke_ref, vb_ref, dec_ref, g_ref, bonus_ref) = refs
    C = RW_CHUNK
    W = RW_W
    slots = RW_LOW[:3] if first else RW_LOW
    nlr = sum(slots)
    has_prev = (pl.program_id(0) > 0).astype(F32)
    row1 = lax.broadcasted_iota(jnp.int32, (tm, 1), 0)

    def shifted(ref, h_ref, c0, c1):
        prev_last = h_ref[HALO - 1:HALO, c0:c1] * has_prev
        return jnp.where(row1 == 0, prev_last, pltpu.roll(ref[:, c0:c1], 1, axis=0))

    mu_r, mu_k, mu_v, w0, a0, k_k, k_a, r_k, v0 = (vec_ref[i:i + 1, :] for i in range(9))
    r_p, k_p, v_p = (p_ref[:, i * W:(i + 1) * W] for i in range(3))
    r = r_p + (shifted(p_ref, halo_ref, 0, W) - r_p) * mu_r
    k = k_p + (shifted(p_ref, halo_ref, W, 2 * W) - k_p) * mu_k
    v = v_p + (shifted(p_ref, halo_ref, 2 * W, 3 * W) - v_p) * mu_v
    low = pl_ref[:, :nlr] + shifted(pl_ref, halo_l_ref, nlr, 2 * nlr)
    o_w, o_a, o_g, o_v = 0, slots[0], slots[0] + slots[1], slots[0] + slots[1] + slots[2]
    w_log = -_softplus(-(w0 + _dot(jnp.tanh(low[:, o_w:o_a]).astype(BF16), w2_ref[...]))) - 0.5
    lw = -jnp.exp(w_log)
    a = _sigmoid(a0 + _dot(low[:, o_a:o_g].astype(BF16), a2_ref[...]))
    g_ref[...] = _dot(_sigmoid(low[:, o_g:o_v]).astype(BF16), g2_ref[...])
    if first:
        vf_out_ref[...] = v
    else:
        v = v + (vf_ref[...] - v) * _sigmoid(v0 + _dot(low[:, o_v:nlr].astype(BF16), v2_ref[...]))

    bd = _head_ones()
    kk = k * k_k
    kk = kk * lax.rsqrt(jnp.maximum(_head_sum(kk * kk, bd), 1e-24))
    k = k * (1.0 + (a - 1.0) * k_a)
    bonus_ref[...] = _head_sum(r * k * r_k, bd) * v
    b_vec = kk * a

    rr = lax.broadcasted_iota(jnp.int32, (tm, tm), 0)
    cc = lax.broadcasted_iota(jnp.int32, (tm, tm), 1)
    same = (rr // C) == (cc // C)
    cum_incl = jnp.where(same & (cc <= rr), 1.0, 0.0).astype(BF16)
    cum_all = jnp.where(same, 1.0, 0.0).astype(BF16)
    lw_hi, lw_lo = _split(lw)
    g_in = _dot(cum_incl, lw_hi) + _dot(cum_incl, lw_lo)
    g_last = _dot(cum_all, lw_hi) + _dot(cum_all, lw_lo)
    rd_ref[...] = (r * jnp.exp(g_in)).astype(BF16)
    ad_ref[...] = (-kk * jnp.exp(g_in - lw)).astype(BF16)
    inv = jnp.exp(-g_in)
    bh_ref[...] = (b_vec * inv).astype(BF16)
    kh_ref[...] = (k * inv).astype(BF16)
    to_end = jnp.exp(g_last - g_in)
    be_ref[...] = (b_vec * to_end).astype(BF16)
    ke_ref[...] = (k * to_end).astype(BF16)
    vb_ref[...] = v.astype(BF16)
    dec_ref[...] = jnp.exp(g_last)


def rwkv_prep(p_rkv, p_low, v_first, vecs, w2, a2, g2, v2, tm=256):
    T = p_rkv.shape[0]
    tm = min(tm, T)
    first = v_first is None
    row = lambda n: pl.BlockSpec((tm, n), lambda i: (i, 0))
    full = lambda arr: pl.BlockSpec(arr.shape, lambda i: (0, 0))
    halo = lambda n: pl.BlockSpec((HALO, n), lambda i: (jnp.maximum(i * (tm // HALO) - 1, 0), 0))
    w_rkv, w_low = p_rkv.shape[1], p_low.shape[1]
    ins = ([p_rkv, p_rkv, p_low, p_low] + ([] if first else [v_first]) + [vecs, w2, a2, g2]
           + ([] if first else [v2]))
    in_specs = ([row(w_rkv), halo(w_rkv), row(w_low), halo(w_low)] + ([] if first else [row(RW_W)])
                + [full(vecs), full(w2), full(a2), full(g2)] + ([] if first else [full(v2)]))
    n_bf, n_f32 = 7, (4 if first else 3)
    outs = pl.pallas_call(
        functools.partial(_rwkv_prep_kernel, tm=tm, first=first),
        grid=(T // tm,),
        in_specs=in_specs,
        out_specs=[row(RW_W)] * (n_bf + n_f32),
        out_shape=[jax.ShapeDtypeStruct((T, RW_W), BF16)] * n_bf + [jax.ShapeDtypeStruct((T, RW_W), F32)] * n_f32,
        compiler_params=_params("parallel"),
        name="rwkv_prep",
    )(*ins)
    return outs


def _rwkv_post_kernel(o_ref, g_ref, bonus_ref, lng_ref, lnb_ref, y_ref):
    bd = _head_ones()
    o = o_ref[...]
    d = o - _head_sum(o, bd) * (1.0 / RW_DH)
    var = _head_sum(d * d, bd) * (1.0 / RW_DH)
    y = d * lax.rsqrt(var + RW_LN_EPS) * lng_ref[...] + lnb_ref[...] + bonus_ref[...]
    y_ref[...] = (y * g_ref[...]).astype(y_ref.dtype)


def rwkv_post(o, g, bonus, ln_g, ln_b, tm=256):
    T = o.shape[0]
    tm = min(tm, T)
    row = pl.BlockSpec((tm, RW_W), lambda i: (i, 0))
    vec = pl.BlockSpec((1, RW_W), lambda i: (0, 0))
    return pl.pallas_call(
        _rwkv_post_kernel,
        grid=(T // tm,),
        in_specs=[row, row, row, vec, vec],
        out_specs=row,
        out_shape=jax.ShapeDtypeStruct((T, RW_W), BF16),
        compiler_params=_params("parallel"),
        name="rwkv_post",
    )(o, g, bonus, ln_g.reshape(1, RW_W), ln_b.reshape(1, RW_W))


def rwkv_time_mix(p_rkv, p_low, v_first, vecs, w2, a2, g2, v2, ln_g, ln_b):
    outs = rwkv_prep(p_rkv, p_low, v_first, vecs, w2, a2, g2, v2)
    rd, ad, bh, kh, be, ke, vb, dec, g, bonus = outs[:10]
    if v_first is None:
        v_first = outs[10]
    o = wkv7(rd, ad, bh, kh, be, ke, vb, dec)
    return rwkv_post(o, g, bonus, ln_g, ln_b), v_first


def _pad_cols(w, mult):
    n = w.shape[1]
    pad = (-n) % mult
    return w if pad == 0 else jnp.pad(w, ((0, 0), (0, pad)))


def kernel(x, mem, ffn1_pre, ffn1_post, ffn1_w1, ffn1_w3, ffn1_w2, mix_pre, mix_post, w_in, gla_a1, gla_a2, gla_ab, gla_norm, rw_mu_rkv, rw_mu_wag, rw_w0, rw_w1, rw_w2, rw_a0, rw_a1, rw_a2, rw_g1, rw_g2, rw_k_k, rw_k_a, rw_r_k, rw_ln_g, rw_ln_b, rw_mu_vl, rw_v0, rw_v1, rw_v2, branch_w, gate_a, gate_c, gate_b, w_out, xa_pre, xa_post, mem_norm, xa_wq, xa_wk, xa_wv, xa_wo, ffn2_pre, ffn2_post, ffn2_w1, ffn2_w3, ffn2_w2):
    B, T, D = x.shape
    depth = ffn1_pre.shape[0]
    bf = lambda t: t.astype(BF16)
    sb_end = 3 * SB_W
    gla_end = sb_end + 2 * GLA_KW + 2 * GLA_VW
    w_in_b = bf(w_in)
    sb_col_scale = jnp.concatenate([jnp.full((SB_W,), SB_QSCALE, F32), jnp.ones((sb_end - SB_W,), F32)])
    outs = []
    for bi in range(B):
        xs = x[bi]
        mem_b = mem[bi]
        h = rmsnorm(xs, ffn1_pre[0])
        v_first = None
        for l in range(depth):
            y = ffn(h, bf(ffn1_w1[l]), bf(ffn1_w3[l]), bf(ffn1_w2[l]))
            xs, h = resid_norm(xs, y, ffn1_post[l], mix_pre[l], 0.5)

            p_sb = mm(h, w_in_b, BF16, layer=l, col0=0, ncols=sb_end, col_scale=sb_col_scale)
            p_gla = mm(h, w_in_b, layer=l, col0=sb_end, ncols=gla_end - sb_end)
            p_rkv = mm(h, w_in_b, layer=l, col0=gla_end, ncols=3 * RW_W)
            lows = [(rw_w1[l], rw_mu_wag[l, 0]), (rw_a1[l], rw_mu_wag[l, 1]), (rw_g1[l], rw_mu_wag[l, 2])]
            if l > 0:
                lows.append((rw_v1[l - 1], rw_mu_vl[l - 1]))
            slot = lambda wl, i: jnp.pad(wl, ((0, 0), (0, RW_LOW[i] - wl.shape[1])))
            w_low = jnp.concatenate([slot(wl * (1.0 - m)[:, None], i) for i, (wl, m) in enumerate(lows)]
                                    + [slot(wl * m[:, None], i) for i, (wl, m) in enumerate(lows)], axis=1)
            gla_ap_col = w_low.shape[1]
            w_low = jnp.concatenate([w_low, _pad_cols(gla_a1[l], LANES)], axis=1)
            p_low = mm(h, bf(_pad_cols(w_low, 1024)))
            ha = mm(h, bf(jnp.concatenate([gate_a[l, g] for g in range(gate_a.shape[1])], axis=1)), BF16)

            y_sb = sb_attention(p_sb)
            a2p = jnp.pad(gla_a2[l], ((0, LANES - gla_a2.shape[1]), (0, 0)))
            y_gla = gla(p_gla, p_low, gla_ap_col // LANES, bf(a2p), gla_ab[l], gla_norm[l])
            v0 = rw_v0[l - 1] if l > 0 else jnp.zeros((RW_W,), F32)
            vec_rows = [rw_mu_rkv[l, 0], rw_mu_rkv[l, 1], rw_mu_rkv[l, 2], rw_w0[l], rw_a0[l], rw_k_k[l], rw_k_a[l],
                        rw_r_k[l].reshape(RW_W), v0]
            vecs = jnp.pad(jnp.stack(vec_rows, axis=0), ((0, RW_VEC_ROWS - len(vec_rows)), (0, 0)))
            pad_rows = lambda w2d, i: jnp.pad(w2d, ((0, RW_LOW[i] - w2d.shape[0]), (0, 0)))
            v2 = bf(pad_rows(rw_v2[l - 1], 3)) if l > 0 else None
            y_rw, v_first = rwkv_time_mix(p_rkv, p_low, v_first, vecs, bf(pad_rows(rw_w2[l], 0)),
                                          bf(pad_rows(rw_a2[l], 1)), bf(pad_rows(rw_g2[l], 2)), v2,
                                          rw_ln_g[l], rw_ln_b[l])

            y = merge(ha, y_sb, y_gla, y_rw, bf(gate_c[l]), gate_b[l], bf(branch_w[l]), bf(w_out[l]))
            xs, h = resid_norm(xs, y, mix_post[l], xa_pre[l], 1.0)

            mem_n = rmsnorm(mem_b, mem_norm[l])
            kx = mm(mem_n, bf(xa_wk[l]), BF16)
            vx = mm(mem_n, bf(xa_wv[l]), BF16)
            xs, h = xattn_resid(h, xs, bf(xa_wq[l]), kx, vx, bf(xa_wo[l]), xa_post[l], ffn2_pre[l])

            y = ffn(h, bf(ffn2_w1[l]), bf(ffn2_w3[l]), bf(ffn2_w2[l]))
            g_next = ffn1_pre[l + 1] if l + 1 < depth else ffn1_pre[0]
            xs, h = resid_norm(xs, y, ffn2_post[l], g_next, 0.5)
        outs.append(xs)
    return jnp.stack(outs, axis=0)
```

```python
import functools
import math

import jax
import jax.numpy as jnp
from jax import lax
from jax.experimental import pallas as pl
from jax.experimental.pallas import tpu as pltpu

F32 = jnp.float32
BF16 = jnp.bfloat16

EPS = 1e-6
CHUNK = 64
SB_HEADS, SB_DH, SB_W = 4, 256, 1024
GLA_HEADS, GLA_DK, GLA_DV, GLA_KW, GLA_VW = 4, 128, 256, 512, 1024
GLA_TAU = 16.0
RW_HEADS, RW_DH, RW_W = 16, 64, 1024
RW_LN_EPS = 64e-5
RW_CHUNK = 16
XA_HEADS, XA_DH, XA_W = 4, 256, 1024
LANES = 128
VMEM_LIMIT = 56 * 1024 * 1024


def _params(*sem):
    return pltpu.CompilerParams(dimension_semantics=sem, vmem_limit_bytes=VMEM_LIMIT)


def _dot(a, b):
    return jnp.dot(a, b, preferred_element_type=F32)


def _dot_nt(a, b):
    return lax.dot_general(a, b, (((1,), (1,)), ((), ())), preferred_element_type=F32)


def _dot_tn(a, b):
    return lax.dot_general(a, b, (((0,), (0,)), ((), ())), preferred_element_type=F32)


def _split(x):
    hi = x.astype(BF16)
    lo = (x - hi.astype(F32)).astype(BF16)
    return hi, lo


def _softplus(z):
    return jnp.maximum(z, 0.0) + jnp.log1p(jnp.exp(-jnp.abs(z)))


def _sigmoid(z):
    return 1.0 / (1.0 + jnp.exp(-z))


def _tile(n, pref):
    t = min(n, pref)
    while n % t:
        t -= LANES
    return t


def _mm_kernel(a_ref, b_ref, o_ref):
    o_ref[...] = _dot(a_ref[...].astype(BF16), b_ref[...]).astype(o_ref.dtype)


def _mm_scaled_kernel(a_ref, b_ref, s_ref, o_ref):
    o_ref[...] = (_dot(a_ref[...].astype(BF16), b_ref[...]) * s_ref[...]).astype(o_ref.dtype)


def mm(a, b, out_dtype=F32, tm=1024, tn=1024, layer=None, col0=0, ncols=None, col_scale=None):
    M, K = a.shape
    N = b.shape[-1] - col0 if ncols is None else ncols
    tm = min(tm, M)
    tn = _tile(math.gcd(N, col0) if col0 else N, tn)
    assert M % tm == 0 and N % tn == 0 and col0 % tn == 0
    c0 = col0 // tn
    if layer is None:
        b_spec = pl.BlockSpec((K, tn), lambda i, j: (0, c0 + j))
    else:
        b_spec = pl.BlockSpec((None, K, tn), lambda i, j: (layer, 0, c0 + j))
    in_specs = [pl.BlockSpec((tm, K), lambda i, j: (i, 0)), b_spec]
    args = [a, b]
    if col_scale is not None:
        in_specs.append(pl.BlockSpec((1, tn), lambda i, j: (0, j)))
        args.append(col_scale.reshape(1, N).astype(F32))
    return pl.pallas_call(
        _mm_kernel if col_scale is None else _mm_scaled_kernel,
        grid=(M // tm, N // tn),
        in_specs=in_specs,
        out_specs=pl.BlockSpec((tm, tn), lambda i, j: (i, j)),
        out_shape=jax.ShapeDtypeStruct((M, N), out_dtype),
        compiler_params=_params("parallel", "parallel"),
        name="mm",
    )(*args)


def _rms(x, g):
    return x * lax.rsqrt(jnp.mean(x * x, axis=-1, keepdims=True) + EPS) * g


def _rmsnorm_kernel(x_ref, g_ref, o_ref):
    o_ref[...] = _rms(x_ref[...], g_ref[...]).astype(o_ref.dtype)


def rmsnorm(x, g, out_dtype=BF16, tm=256):
    M, D = x.shape
    tm = min(tm, M)
    return pl.pallas_call(
        _rmsnorm_kernel,
        grid=(M // tm,),
        in_specs=[pl.BlockSpec((tm, D), lambda i: (i, 0)),
                  pl.BlockSpec((1, D), lambda i: (0, 0))],
        out_specs=pl.BlockSpec((tm, D), lambda i: (i, 0)),
        out_shape=jax.ShapeDtypeStruct((M, D), out_dtype),
        compiler_params=_params("parallel"),
        name="rmsnorm",
    )(x, g.reshape(1, D))


def _resid_norm_kernel(x_ref, y_ref, gp_ref, gn_ref, xo_ref, h_ref, *, coef):
    xn = x_ref[...] + coef * _rms(y_ref[...], gp_ref[...])
    xo_ref[...] = xn
    h_ref[...] = _rms(xn, gn_ref[...]).astype(h_ref.dtype)


def resid_norm(x, y, g_post, g_next, coef, tm=256):
    M, D = x.shape
    tm = min(tm, M)
    row = pl.BlockSpec((tm, D), lambda i: (i, 0))
    vec = pl.BlockSpec((1, D), lambda i: (0, 0))
    return pl.pallas_call(
        functools.partial(_resid_norm_kernel, coef=coef),
        grid=(M // tm,),
        in_specs=[row, row, vec, vec],
        out_specs=[row, row],
        out_shape=[jax.ShapeDtypeStruct((M, D), F32), jax.ShapeDtypeStruct((M, D), BF16)],
        compiler_params=_params("parallel"),
        name="resid_norm",
    )(x, y, g_post.reshape(1, D), g_next.reshape(1, D))


def _ffn_up_kernel(h_ref, w1_ref, w3_ref, u_ref):
    h = h_ref[...]
    a = _dot(h, w1_ref[...])
    b = _dot(h, w3_ref[...])
    u_ref[...] = (a * _sigmoid(a) * b).astype(u_ref.dtype)


def ffn_up(h, w1, w3, layer, tm=1024, tf=512):
    M, D = h.shape
    Fd = w1.shape[-1]
    tm = min(tm, M)
    tf = min(tf, Fd)
    wspec = pl.BlockSpec((None, D, tf), lambda i, j: (layer, 0, j))
    return pl.pallas_call(
        _ffn_up_kernel,
        grid=(M // tm, Fd // tf),
        in_specs=[pl.BlockSpec((tm, D), lambda i, j: (i, 0)), wspec, wspec],
        out_specs=pl.BlockSpec((tm, tf), lambda i, j: (i, j)),
        out_shape=jax.ShapeDtypeStruct((M, Fd), BF16),
        compiler_params=_params("parallel", "parallel"),
        name="ffn_up",
    )(h, w1, w3)


def _mm_resid_kernel(a_ref, w_ref, x_ref, gp_ref, gn_ref, xo_ref, ho_ref, *, coef):
    y = _dot(a_ref[...], w_ref[...])
    xn = x_ref[...] + coef * _rms(y, gp_ref[...])
    xo_ref[...] = xn
    ho_ref[...] = _rms(xn, gn_ref[...]).astype(ho_ref.dtype)


def mm_resid(a, w, layer, x, g_post, g_next, coef, tm=256):
    M, K = a.shape
    D = w.shape[-1]
    tm = min(tm, M)
    row = pl.BlockSpec((tm, D), lambda i: (i, 0))
    vec = pl.BlockSpec((1, D), lambda i: (0, 0), pipeline_mode=pl.Buffered(1))
    return pl.pallas_call(
        functools.partial(_mm_resid_kernel, coef=coef),
        grid=(M // tm,),
        in_specs=[pl.BlockSpec((tm, K), lambda i: (i, 0)),
                  pl.BlockSpec((None, K, D), lambda i: (layer, 0, 0), pipeline_mode=pl.Buffered(1)),
                  row, vec, vec],
        out_specs=[row, row],
        out_shape=[jax.ShapeDtypeStruct((M, D), F32), jax.ShapeDtypeStruct((M, D), BF16)],
        compiler_params=_params("parallel"),
        name="mm_resid",
    )(a, w, x, g_post.reshape(1, D), g_next.reshape(1, D))


def _merge_kernel(ha_ref, ysb_ref, ygl_ref, yrw_ref, gc_ref, gb_ref, bw_ref, wo_ref, o_ref, *, rank):
    ha = ha_ref[...]
    m = None
    for g, y_ref in enumerate((ysb_ref, ygl_ref, yrw_ref)):
        gate = _sigmoid(_dot(ha[:, g * rank:(g + 1) * rank], gc_ref[g]) + gb_ref[g])
        term = gate * _dot(y_ref[...], bw_ref[g])
        m = term if m is None else m + term
    part = _dot(m.astype(BF16), wo_ref[...])

    @pl.when(pl.program_id(1) == 0)
    def _():
        o_ref[...] = part

    @pl.when(pl.program_id(1) != 0)
    def _():
        o_ref[...] += part


def merge(ha, y_sb, y_gla, y_rw, gate_c, gate_b, branch_w, w_out, tm=512, tn=512):
    M = ha.shape[0]
    nb, rank, D = gate_c.shape
    W = branch_w.shape[1]
    tm = min(tm, M)
    tn = min(tn, D)
    ybs = pl.BlockSpec((tm, W), lambda i, j: (i, 0))
    return pl.pallas_call(
        functools.partial(_merge_kernel, rank=rank),
        grid=(M // tm, D // tn),
        in_specs=[pl.BlockSpec((tm, nb * rank), lambda i, j: (i, 0)), ybs, ybs, ybs,
                  pl.BlockSpec((nb, rank, tn), lambda i, j: (0, 0, j)),
                  pl.BlockSpec((nb, 1, tn), lambda i, j: (0, 0, j)),
                  pl.BlockSpec((nb, W, tn), lambda i, j: (0, 0, j)),
                  pl.BlockSpec((tn, D), lambda i, j: (j, 0))],
        out_specs=pl.BlockSpec((tm, D), lambda i, j: (i, 0)),
        out_shape=jax.ShapeDtypeStruct((M, D), F32),
        compiler_params=_params("parallel", "arbitrary"),
        name="merge",
    )(ha, y_sb, y_gla, y_rw, gate_c, gate_b.reshape(nb, 1, D), branch_w, w_out)


def _xattn_kernel(h_ref, x_ref, wq_ref, k_ref, v_ref, wo_ref, gp_ref, gn_ref, xo_ref, ho_ref):
    q = _dot(h_ref[...], wq_ref[...])
    outs = []
    for hh in range(XA_HEADS):
        sl = slice(hh * XA_DH, (hh + 1) * XA_DH)
        s = _dot_nt(q[:, sl].astype(BF16), k_ref[:, sl]) * (XA_DH ** -0.5)
        e = jnp.exp(s - jnp.max(s, axis=-1, keepdims=True))
        p = e / jnp.sum(e, axis=-1, keepdims=True)
        outs.append(_dot(p.astype(BF16), v_ref[:, sl]).astype(BF16))
    y = _dot(jnp.concatenate(outs, axis=1), wo_ref[...])
    xn = x_ref[...] + _rms(y, gp_ref[...])
    xo_ref[...] = xn
    ho_ref[...] = _rms(xn, gn_ref[...]).astype(ho_ref.dtype)


def xattn_resid(h, x, wq, k, v, wo, g_post, g_next, tm=256):
    M, D = h.shape
    tm = min(tm, M)
    row = pl.BlockSpec((tm, D), lambda i: (i, 0))
    whole = lambda arr: pl.BlockSpec(arr.shape, lambda i: (0, 0), pipeline_mode=pl.Buffered(1))
    gp, gn = g_post.reshape(1, D), g_next.reshape(1, D)
    return pl.pallas_call(
        _xattn_kernel,
        grid=(M // tm,),
        in_specs=[row, row, whole(wq), whole(k), whole(v), whole(wo), whole(gp), whole(gn)],
        out_specs=[row, row],
        out_shape=[jax.ShapeDtypeStruct((M, D), F32), jax.ShapeDtypeStruct((M, D), BF16)],
        compiler_params=_params("parallel"),
        name="xattn_resid",
    )(h, x, wq, k, v, wo, gp, gn)


SB_SUB = 256
SB_ROWS = 256
SB_QSCALE = SB_DH ** -0.5 * math.log2(math.e)


def _sb_kernel(qi_tab, kj_tab, q_ref, k_ref, v_ref, o_ref, acc_ref, carry_ref, *, tb):
    step = pl.program_id(1)
    qi = qi_tab[step]
    kj = kj_tab[step]
    sub = min(SB_SUB, tb)
    nsub = tb // sub

    @pl.when(kj == qi)
    def _():
        acc_ref[...] = jnp.zeros_like(acc_ref)
        carry_ref[...] = jnp.zeros_like(carry_ref)

    def body(diag):
        jj = lax.broadcasted_iota(jnp.int32, (sub, sub), 0)
        ss = lax.broadcasted_iota(jnp.int32, (sub, sub), 1)
        neg_tri = jnp.where(jj > ss, -1.0, 0.0).astype(BF16)
        rb_rows = min(SB_ROWS, tb)

        def row_block(rb):
            r0 = rb * rb_rows
            rs = slice(r0, r0 + rb_rows)
            z_rb = _dot_nt(q_ref[rs, :], k_ref[...])
            yield
            carry = carry_ref[rs, :]
            a_parts = [None] * nsub
            for c in reversed(range(nsub)):
                c0 = c * sub
                if diag and r0 + rb_rows - 1 <= c0:
                    a_parts[c] = jnp.zeros((rb_rows, sub), BF16)
                    continue
                masked = diag and r0 <= c0 + sub - 1
                z = z_rb[:, c0:c0 + sub]
                neg_abs = lax.bitcast_convert_type(
                    lax.bitcast_convert_type(z, jnp.uint32) | jnp.uint32(0x80000000), F32)
                sp = jnp.maximum(z, 0.0) + jnp.log2(1.0 + jnp.exp2(neg_abs))
                if masked:
                    row = lax.broadcasted_iota(jnp.int32, z.shape, 0) + r0
                    col = lax.broadcasted_iota(jnp.int32, z.shape, 1) + c0
                    causal = col < row
                    sp = jnp.where(causal, sp, 0.0)
                spb = sp.astype(BF16)
                yield
                later = _dot(spb, neg_tri) + carry
                yield
                a = jnp.exp2(z - sp + later)
                if masked:
                    a = jnp.where(causal, a, 0.0)
                a_parts[c] = a.astype(BF16)
                carry = later[:, 0:1] - sp[:, 0:1]
                yield
            carry_ref[rs, :] = carry
            acc_ref[rs, :] += _dot(jnp.concatenate(a_parts, axis=1), v_ref[...])

        nrb = tb // rb_rows
        gens = [row_block(rb) for rb in range(nrb)]
        live = []
        pending = list(range(nrb))
        while pending or live:
            if pending:
                live.append(gens[pending.pop(0)])
            for g in list(live):
                try:
                    next(g)
                except StopIteration:
                    live.remove(g)

    @pl.when(kj == qi)
    def _():
        body(True)

    @pl.when(kj != qi)
    def _():
        body(False)

    @pl.when(kj == 0)
    def _():
        o_ref[...] = acc_ref[...].astype(o_ref.dtype)


def sb_attention(p_sb, tb=2048):
    T = p_sb.shape[0]
    tb = min(tb, T)
    nq = T // tb
    qi_l, kj_l = [], []
    for i in range(nq):
        for j in range(i, -1, -1):
            qi_l.append(i)
            kj_l.append(j)
    qi_tab = jnp.asarray(qi_l, jnp.int32)
    kj_tab = jnp.asarray(kj_l, jnp.int32)
    H = SB_HEADS
    grid_spec = pltpu.PrefetchScalarGridSpec(
        num_scalar_prefetch=2,
        grid=(H, len(qi_l)),
        in_specs=[pl.BlockSpec((tb, SB_DH), lambda h, s, qt, kt: (qt[s], h)),
                  pl.BlockSpec((tb, SB_DH), lambda h, s, qt, kt: (kt[s], H + h)),
                  pl.BlockSpec((tb, SB_DH), lambda h, s, qt, kt: (kt[s], 2 * H + h))],
        out_specs=pl.BlockSpec((tb, SB_DH), lambda h, s, qt, kt: (qt[s], h)),
        scratch_shapes=[pltpu.VMEM((tb, SB_DH), F32), pltpu.VMEM((tb, 1), F32)],
    )
    return pl.pallas_call(
        functools.partial(_sb_kernel, tb=tb),
        grid_spec=grid_spec,
        out_shape=jax.ShapeDtypeStruct((T, SB_W), BF16),
        compiler_params=_params("parallel", "arbitrary"),
        name="sb_attention",
    )(qi_tab, kj_tab, p_sb, p_sb, p_sb)


def _gla_kernel(q_ref, k_ref, v_ref, g_ref, ap_ref, a2_ref, ab_ref, ng_ref, o_ref, s_ref, *, tb):
    C = CHUNK
    nch = tb // C

    @pl.when(pl.program_id(0) == 0)
    def _():
        s_ref[...] = jnp.zeros_like(s_ref)

    rr = lax.broadcasted_iota(jnp.int32, (tb, tb), 0)
    cc = lax.broadcasted_iota(jnp.int32, (tb, tb), 1)
    same = (rr // C) == (cc // C)
    cum_incl = jnp.where(same & (cc <= rr), 1.0, 0.0).astype(BF16)
    cum_all = jnp.where(same, 1.0, 0.0).astype(BF16)
    pre = _dot(ap_ref[...].astype(BF16), a2_ref[...]) + ab_ref[...]
    la = -_softplus(-pre) * (1.0 / GLA_TAU)
    hi, lo = _split(la)
    b = _dot(cum_incl, hi) + _dot(cum_incl, lo)
    b_last = _dot(cum_all, hi) + _dot(cum_all, lo)
    k = k_ref[...]
    q_in = (q_ref[...] * (GLA_DK ** -0.5) * jnp.exp(b)).astype(BF16)
    k_in = (k * jnp.exp(-b)).astype(BF16)
    k_end = (k * jnp.exp(b_last - b)).astype(BF16)
    dec = jnp.exp(b_last)

    tt = lax.broadcasted_iota(jnp.int32, (C, C), 0)
    ss = lax.broadcasted_iota(jnp.int32, (C, C), 1)
    causal = ss <= tt

    def head(hh):
        kc = slice(hh * GLA_DK, (hh + 1) * GLA_DK)
        vc = slice(hh * GLA_DV, (hh + 1) * GLA_DV)
        st = s_ref[hh]
        for c in range(nch):
            sl = slice(c * C, (c + 1) * C)
            qi = q_in[sl, kc]
            v = v_ref[sl, vc].astype(BF16)
            scores = jnp.where(causal, _dot_nt(qi, k_in[sl, kc]), 0.0)
            kv = _dot_tn(v, k_end[sl, kc])
            yield
            o = _dot(scores.astype(BF16), v) + _dot_nt(qi, st.astype(BF16))
            st = dec[c * C:c * C + 1, kc] * st + kv
            yield
            o = o * lax.rsqrt(jnp.mean(o * o, axis=-1, keepdims=True) + EPS) * ng_ref[...]
            g = g_ref[sl, vc]
            o_ref[sl, vc] = (o * (g * _sigmoid(g))).astype(o_ref.dtype)
        s_ref[hh] = st

    live = [head(hh) for hh in range(GLA_HEADS)]
    while live:
        for gen in list(live):
            try:
                next(gen)
            except StopIteration:
                live.remove(gen)


def gla(p_gla, p_ap, ap_blk, a2p, ab, norm_g, tb=512):
    T = p_gla.shape[0]
    tb = min(tb, T)
    assert 2 * GLA_KW == GLA_VW
    return pl.pallas_call(
        functools.partial(_gla_kernel, tb=tb),
        grid=(T // tb,),
        in_specs=[pl.BlockSpec((tb, GLA_KW), lambda i: (i, 0)),
                  pl.BlockSpec((tb, GLA_KW), lambda i: (i, 1)),
                  pl.BlockSpec((tb, GLA_VW), lambda i: (i, 1)),
                  pl.BlockSpec((tb, GLA_VW), lambda i: (i, 2)),
                  pl.BlockSpec((tb, LANES), lambda i: (i, ap_blk)),
                  pl.BlockSpec((LANES, GLA_KW), lambda i: (0, 0)),
                  pl.BlockSpec((1, GLA_KW), lambda i: (0, 0)),
                  pl.BlockSpec((1, GLA_DV), lambda i: (0, 0))],
        out_specs=pl.BlockSpec((tb, GLA_VW), lambda i: (i, 0)),
        out_shape=jax.ShapeDtypeStruct((T, GLA_VW), BF16),
        scratch_shapes=[pltpu.VMEM((GLA_HEADS, GLA_DV, GLA_DK), F32)],
        compiler_params=_params("arbitrary"),
        name="gla",
    )(p_gla, p_gla, p_gla, p_gla, p_ap, a2p, ab.reshape(1, GLA_KW), norm_g.reshape(1, GLA_DV))


def _wkv_kernel(rd_ref, ad_ref, bh_ref, kh_ref, be_ref, ke_ref, v_ref, dec_ref, o_ref,
                s_ref, rhs_s, uvt_s, pc_s, bc_s, mrb_s, oloc_s, *, R, NS):
    C = RW_CHUNK
    N = RW_DH
    HB = RW_HEADS
    nch = R // C
    j = pl.program_id(0)

    @pl.when(j == 0)
    def _():
        for ref in (s_ref, rhs_s, uvt_s, pc_s, bc_s, mrb_s, oloc_s):
            ref[...] = jnp.zeros_like(ref)

    w_base = (j % 2) * (NS * HB)
    r_base = (1 - j % 2) * (NS * HB)

    row = lax.broadcasted_iota(jnp.int32, (R, R), 0)
    col = lax.broadcasted_iota(jnp.int32, (R, R), 1)
    same = (row // C) == (col // C)
    m_incl = same & (col <= row)
    m_strict = same & (col < row)
    m_incl_t = same & (row <= col)
    one = lambda m: jnp.where(m, 1.0, 0.0).astype(BF16)
    eye_r = one(row == col)
    eye_rf = jnp.where(row == col, 1.0, 0.0)
    ri = lax.broadcasted_iota(jnp.int32, (N, N), 0)
    ci = lax.broadcasted_iota(jnp.int32, (N, N), 1)
    eye_n = one(ri == ci)
    lane_chunk = lax.broadcasted_iota(jnp.int32, (N, R), 1) // C
    steps = C.bit_length() - 2

    def prepare(ti, hd):
        rows = slice(ti * R, (ti + 1) * R)
        lanes = slice(hd * N, (hd + 1) * N)
        r_d, a_d, b_h, k_h, b_e, k_e, v = (ref[rows, lanes] for ref in
                                           (rd_ref, ad_ref, bh_ref, kh_ref, be_ref, ke_ref, v_ref))
        gram = _dot_nt(jnp.concatenate([a_d, r_d], axis=0), jnp.concatenate([b_h, k_h], axis=0))
        l_ab = jnp.where(m_strict, gram[:R, :R], 0.0)
        l_ak = jnp.where(m_strict, gram[:R, R:], 0.0).astype(BF16)
        m_rk = jnp.where(m_incl, gram[R:, R:], 0.0).astype(BF16)
        m_rb_t = jnp.where(m_incl_t, _dot_nt(b_h, r_d), 0.0).astype(BF16)
        xt = _dot_nt(eye_n, jnp.concatenate([v, a_d, r_d], axis=0))
        v_t = xt[:, :R].astype(BF16)
        a_dt = xt[:, R:2 * R].astype(BF16)
        r_dt = xt[:, 2 * R:].astype(BF16)
        yield

        inv_m = eye_rf + l_ab
        p = l_ab
        lv_t = _dot_nt(v_t, l_ak).astype(BF16)
        o_loct = _dot_nt(v_t, m_rk)
        for _ in range(steps):
            pb = p.astype(BF16)
            p = _dot(pb, pb)
            yield
            inv_m = inv_m + _dot(inv_m.astype(BF16), p.astype(BF16))
            yield
        inv_b = inv_m.astype(BF16)
        w_at = _dot_nt(a_dt, inv_b).astype(BF16)
        u_vt = _dot_nt(lv_t, inv_b)
        yield
        stack = lambda x: jnp.concatenate(
            [jnp.where(lane_chunk == c, x, jnp.zeros_like(x)) for c in range(nch)], axis=0)
        p_c = _dot(stack(w_at), b_e).astype(BF16)
        b_c = _dot(jnp.concatenate([stack(u_vt.astype(BF16)), stack(v_t)], axis=1),
                   jnp.concatenate([b_e, k_e], axis=0))
        iw = w_base + ti * HB + hd
        rhs_s[iw] = jnp.concatenate([w_at, r_dt], axis=1)
        uvt_s[iw] = u_vt
        pc_s[iw] = p_c
        bc_s[iw] = b_c
        mrb_s[iw] = m_rb_t
        oloc_s[iw] = o_loct

    st = [s_ref[hd] for hd in range(HB)]

    def scan(ti):
        entering = [[] for _ in range(HB)]
        for c in range(nch):
            for hd in range(HB):
                ir = r_base + ti * HB + hd
                sb = st[hd].astype(BF16)
                entering[hd].append(sb)
                dec = dec_ref[ti * R + c * C:ti * R + c * C + 1, hd * N:(hd + 1) * N]
                st[hd] = dec * st[hd] + _dot(sb, pc_s[ir, c * N:(c + 1) * N, :]) + bc_s[ir, c * N:(c + 1) * N, :]
            yield

        def outputs(hd):
            ir = r_base + ti * HB + hd
            res = _dot(jnp.concatenate(entering[hd], axis=0), rhs_s[ir])
            yield
            u_t = uvt_s[ir]
            o_t = oloc_s[ir]
            for c in range(nch):
                m = lane_chunk == c
                u_t = u_t + jnp.where(m, res[c * N:(c + 1) * N, :R], 0.0)
                o_t = o_t + jnp.where(m, res[c * N:(c + 1) * N, R:], 0.0)
            o_t = o_t + _dot(u_t.astype(BF16), mrb_s[ir])
            yield
            hi, lo = _split(o_t)
            o_ref[ti * R:(ti + 1) * R, hd * N:(hd + 1) * N] = _dot_nt(eye_r, hi) + _dot_nt(eye_r, lo)

        outs = [outputs(hd) for hd in range(HB)]
        while outs:
            for gen in list(outs):
                try:
                    next(gen)
                except StopIteration:
                    outs.remove(gen)
            yield

    def scans():
        for ti in range(NS):
            yield from scan(ti)

    live = [scans()] + [prepare(ti, hd) for ti in range(NS) for hd in range(HB)]
    while live:
        for gen in list(live):
            try:
                next(gen)
            except StopIteration:
                live.remove(gen)
    for hd in range(HB):
        s_ref[hd] = st[hd]


def wkv7(rd, ad, bh, kh, be, ke, vb, dec, R=128, NS=1):
    T = rd.shape[0]
    R = min(R, T)
    NS = min(NS, T // R)
    nblk = T // (NS * R)
    N, H = RW_DH, RW_HEADS
    nbuf = 2 * NS * H
    prep_spec = pl.BlockSpec((NS * R, RW_W), lambda j: (jnp.minimum(j, nblk - 1), 0))
    scan_spec = pl.BlockSpec((NS * R, RW_W), lambda j: (jnp.maximum(j - 1, 0), 0))
    return pl.pallas_call(
        functools.partial(_wkv_kernel, R=R, NS=NS),
        grid=(nblk + 1,),
        in_specs=[prep_spec] * 7 + [scan_spec],
        out_specs=scan_spec,
        out_shape=jax.ShapeDtypeStruct((T, RW_W), F32),
        scratch_shapes=[pltpu.VMEM((H, N, N), F32),
                        pltpu.VMEM((nbuf, N, 2 * R), BF16), pltpu.VMEM((nbuf, N, R), F32),
                        pltpu.VMEM((nbuf, (R // RW_CHUNK) * N, N), BF16),
                        pltpu.VMEM((nbuf, (R // RW_CHUNK) * N, N), F32),
                        pltpu.VMEM((nbuf, R, R), BF16), pltpu.VMEM((nbuf, N, R), F32)],
        compiler_params=_params("arbitrary"),
        name="wkv7",
    )(rd, ad, bh, kh, be, ke, vb, dec)


RW_LOW = (128, 128, 512, 128)
RW_VEC_ROWS = 16
HALO = 8


def _head_sum(x, bd):
    m = x.shape[0]
    hi = x.astype(BF16)
    r1 = x - hi.astype(F32)
    mid = r1.astype(BF16)
    lo = (r1 - mid.astype(F32)).astype(BF16)
    pieces = jnp.concatenate([hi, mid, lo], axis=0)
    outs = []
    for j in range(RW_W // LANES):
        s = _dot(pieces[:, j * LANES:(j + 1) * LANES], bd)
        outs.append(s[:m] + s[m:2 * m] + s[2 * m:])
    return jnp.concatenate(outs, axis=1)


def _head_ones():
    ri = lax.broadcasted_iota(jnp.int32, (LANES, LANES), 0)
    ci = lax.broadcasted_iota(jnp.int32, (LANES, LANES), 1)
    return jnp.where((ri // RW_DH) == (ci // RW_DH), 1.0, 0.0).astype(BF16)


def _rwkv_prep_kernel(*refs, tm, first):
    if first:
        (p_ref, halo_ref, pl_ref, halo_l_ref, vec_ref, w2_ref, a2_ref, g2_ref,
         rd_ref, ad_ref, bh_ref, kh_ref, be_ref, ke_ref, vb_ref, dec_ref, g_ref, bonus_ref, vf_out_ref) = refs
    else:
        (p_ref, halo_ref, pl_ref, halo_l_ref, vf_ref, vec_ref, w2_ref, a2_ref, g2_ref, v2_ref,
         rd_ref, ad_ref, bh_ref, kh_ref, be_ref, ke_ref, vb_ref, dec_ref, g_ref, bonus_ref) = refs
    C = RW_CHUNK
    W = RW_W
    slots = RW_LOW[:3] if first else RW_LOW
    nlr = sum(slots)
    has_prev = (pl.program_id(0) > 0).astype(F32)
    row1 = lax.broadcasted_iota(jnp.int32, (tm, 1), 0)

    def shifted(ref, h_ref, c0, c1):
        prev_last = h_ref[HALO - 1:HALO, c0:c1] * has_prev
        return jnp.where(row1 == 0, prev_last, pltpu.roll(ref[:, c0:c1], 1, axis=0))

    mu_r, mu_k, mu_v, w0, a0, k_k, k_a, r_k, v0 = (vec_ref[i:i + 1, :] for i in range(9))
    r_p, k_p, v_p = (p_ref[:, i * W:(i + 1) * W] for i in range(3))
    r = r_p + (shifted(p_ref, halo_ref, 0, W) - r_p) * mu_r
    k = k_p + (shifted(p_ref, halo_ref, W, 2 * W) - k_p) * mu_k
    v = v_p + (shifted(p_ref, halo_ref, 2 * W, 3 * W) - v_p) * mu_v
    low = pl_ref[:, :nlr] + shifted(pl_ref, halo_l_ref, nlr, 2 * nlr)
    o_w, o_a, o_g, o_v = 0, slots[0], slots[0] + slots[1], slots[0] + slots[1] + slots[2]
    w_log = -_softplus(-(w0 + _dot(jnp.tanh(low[:, o_w:o_a]).astype(BF16), w2_ref[...]))) - 0.5
    lw = -jnp.exp(w_log)
    a = _sigmoid(a0 + _dot(low[:, o_a:o_g].astype(BF16), a2_ref[...]))
    g_ref[...] = _dot(_sigmoid(low[:, o_g:o_v]).astype(BF16), g2_ref[...])
    if first:
        vf_out_ref[...] = v
    else:
        v = v + (vf_ref[...] - v) * _sigmoid(v0 + _dot(low[:, o_v:nlr].astype(BF16), v2_ref[...]))

    bd = _head_ones()
    kk = k * k_k
    kk = kk * lax.rsqrt(jnp.maximum(_head_sum(kk * kk, bd), 1e-24))
    k = k * (1.0 + (a - 1.0) * k_a)
    bonus_ref[...] = _head_sum(r * k * r_k, bd) * v
    b_vec = kk * a

    rr = lax.broadcasted_iota(jnp.int32, (tm, tm), 0)
    cc = lax.broadcasted_iota(jnp.int32, (tm, tm), 1)
    same = (rr // C) == (cc // C)
    cum_incl = jnp.where(same & (cc <= rr), 1.0, 0.0).astype(BF16)
    cum_all = jnp.where(same, 1.0, 0.0).astype(BF16)
    lw_hi, lw_lo = _split(lw)
    g_in = _dot(cum_incl, lw_hi) + _dot(cum_incl, lw_lo)
    g_last = _dot(cum_all, lw_hi) + _dot(cum_all, lw_lo)
    rd_ref[...] = (r * jnp.exp(g_in)).astype(BF16)
    ad_ref[...] = (-kk * jnp.exp(g_in - lw)).astype(BF16)
    inv = jnp.exp(-g_in)
    bh_ref[...] = (b_vec * inv).astype(BF16)
    kh_ref[...] = (k * inv).astype(BF16)
    to_end = jnp.exp(g_last - g_in)
    be_ref[...] = (b_vec * to_end).astype(BF16)
    ke_ref[...] = (k * to_end).astype(BF16)
    vb_ref[...] = v.astype(BF16)
    dec_ref[...] = jnp.exp(g_last)


def rwkv_prep(p_rkv, p_low, v_first, vecs, w2, a2, g2, v2, tm=256):
    T = p_rkv.shape[0]
    tm = min(tm, T)
    first = v_first is None
    row = lambda n: pl.BlockSpec((tm, n), lambda i: (i, 0))
    full = lambda arr: pl.BlockSpec(arr.shape, lambda i: (0, 0))
    halo = lambda n: pl.BlockSpec((HALO, n), lambda i: (jnp.maximum(i * (tm // HALO) - 1, 0), 0))
    w_rkv, w_low = p_rkv.shape[1], p_low.shape[1]
    ins = ([p_rkv, p_rkv, p_low, p_low] + ([] if first else [v_first]) + [vecs, w2, a2, g2]
           + ([] if first else [v2]))
    in_specs = ([row(w_rkv), halo(w_rkv), row(w_low), halo(w_low)] + ([] if first else [row(RW_W)])
                + [full(vecs), full(w2), full(a2), full(g2)] + ([] if first else [full(v2)]))
    n_bf, n_f32 = 7, (4 if first else 3)
    outs = pl.pallas_call(
        functools.partial(_rwkv_prep_kernel, tm=tm, first=first),
        grid=(T // tm,),
        in_specs=in_specs,
        out_specs=[row(RW_W)] * (n_bf + n_f32),
        out_shape=[jax.ShapeDtypeStruct((T, RW_W), BF16)] * n_bf + [jax.ShapeDtypeStruct((T, RW_W), F32)] * n_f32,
        compiler_params=_params("parallel"),
        name="rwkv_prep",
    )(*ins)
    return outs


def _rwkv_post_kernel(o_ref, g_ref, bonus_ref, lng_ref, lnb_ref, y_ref):
    bd = _head_ones()
    o = o_ref[...]
    d = o - _head_sum(o, bd) * (1.0 / RW_DH)
    var = _head_sum(d * d, bd) * (1.0 / RW_DH)
    y = d * lax.rsqrt(var + RW_LN_EPS) * lng_ref[...] + lnb_ref[...] + bonus_ref[...]
    y_ref[...] = (y * g_ref[...]).astype(y_ref.dtype)


def rwkv_post(o, g, bonus, ln_g, ln_b, tm=256):
    T = o.shape[0]
    tm = min(tm, T)
    row = pl.BlockSpec((tm, RW_W), lambda i: (i, 0))
    vec = pl.BlockSpec((1, RW_W), lambda i: (0, 0))
    return pl.pallas_call(
        _rwkv_post_kernel,
        grid=(T // tm,),
        in_specs=[row, row, row, vec, vec],
        out_specs=row,
        out_shape=jax.ShapeDtypeStruct((T, RW_W), BF16),
        compiler_params=_params("parallel"),
        name="rwkv_post",
    )(o, g, bonus, ln_g.reshape(1, RW_W), ln_b.reshape(1, RW_W))


def rwkv_time_mix(p_rkv, p_low, v_first, vecs, w2, a2, g2, v2, ln_g, ln_b):
    outs = rwkv_prep(p_rkv, p_low, v_first, vecs, w2, a2, g2, v2)
    rd, ad, bh, kh, be, ke, vb, dec, g, bonus = outs[:10]
    if v_first is None:
        v_first = outs[10]
    o = wkv7(rd, ad, bh, kh, be, ke, vb, dec)
    return rwkv_post(o, g, bonus, ln_g, ln_b), v_first


def _pad_cols(w, mult):
    n = w.shape[1]
    pad = (-n) % mult
    return w if pad == 0 else jnp.pad(w, ((0, 0), (0, pad)))


def kernel(x, mem, ffn1_pre, ffn1_post, ffn1_w1, ffn1_w3, ffn1_w2, mix_pre, mix_post, w_in, gla_a1, gla_a2, gla_ab, gla_norm, rw_mu_rkv, rw_mu_wag, rw_w0, rw_w1, rw_w2, rw_a0, rw_a1, rw_a2, rw_g1, rw_g2, rw_k_k, rw_k_a, rw_r_k, rw_ln_g, rw_ln_b, rw_mu_vl, rw_v0, rw_v1, rw_v2, branch_w, gate_a, gate_c, gate_b, w_out, xa_pre, xa_post, mem_norm, xa_wq, xa_wk, xa_wv, xa_wo, ffn2_pre, ffn2_post, ffn2_w1, ffn2_w3, ffn2_w2):
    B, T, D = x.shape
    depth = ffn1_pre.shape[0]
    bf = lambda t: t.astype(BF16)
    sb_end = 3 * SB_W
    gla_end = sb_end + 2 * GLA_KW + 2 * GLA_VW
    w_in_b = bf(w_in)
    ffn_w = [bf(t) for t in (ffn1_w1, ffn1_w3, ffn1_w2, ffn2_w1, ffn2_w3, ffn2_w2)]
    sb_col_scale = jnp.concatenate([jnp.full((SB_W,), SB_QSCALE, F32), jnp.ones((sb_end - SB_W,), F32)])
    outs = []
    for bi in range(B):
        xs = x[bi]
        mem_b = mem[bi]
        h = rmsnorm(xs, ffn1_pre[0])
        v_first = None
        for l in range(depth):
            u = ffn_up(h, ffn_w[0], ffn_w[1], l)
            xs, h = mm_resid(u, ffn_w[2], l, xs, ffn1_post[l], mix_pre[l], 0.5)

            p_sb = mm(h, w_in_b, BF16, layer=l, col0=0, ncols=sb_end, col_scale=sb_col_scale)
            p_gla = mm(h, w_in_b, layer=l, col0=sb_end, ncols=gla_end - sb_end)
            p_rkv = mm(h, w_in_b, layer=l, col0=gla_end, ncols=3 * RW_W)
            lows = [(rw_w1[l], rw_mu_wag[l, 0]), (rw_a1[l], rw_mu_wag[l, 1]), (rw_g1[l], rw_mu_wag[l, 2])]
            if l > 0:
                lows.append((rw_v1[l - 1], rw_mu_vl[l - 1]))
            slot = lambda wl, i: jnp.pad(wl, ((0, 0), (0, RW_LOW[i] - wl.shape[1])))
            w_low = jnp.concatenate([slot(wl * (1.0 - m)[:, None], i) for i, (wl, m) in enumerate(lows)]
                                    + [slot(wl * m[:, None], i) for i, (wl, m) in enumerate(lows)], axis=1)
            gla_ap_col = w_low.shape[1]
            w_low = jnp.concatenate([w_low, _pad_cols(gla_a1[l], LANES)], axis=1)
            p_low = mm(h, bf(_pad_cols(w_low, 1024)))
            ha = mm(h, bf(jnp.concatenate([gate_a[l, g] for g in range(gate_a.shape[1])], axis=1)), BF16)

            y_sb = sb_attention(p_sb)
            a2p = jnp.pad(gla_a2[l], ((0, LANES - gla_a2.shape[1]), (0, 0)))
            y_gla = gla(p_gla, p_low, gla_ap_col // LANES, bf(a2p), gla_ab[l], gla_norm[l])
            v0 = rw_v0[l - 1] if l > 0 else jnp.zeros((RW_W,), F32)
            vec_rows = [rw_mu_rkv[l, 0], rw_mu_rkv[l, 1], rw_mu_rkv[l, 2], rw_w0[l], rw_a0[l], rw_k_k[l], rw_k_a[l],
                        rw_r_k[l].reshape(RW_W), v0]
            vecs = jnp.pad(jnp.stack(vec_rows, axis=0), ((0, RW_VEC_ROWS - len(vec_rows)), (0, 0)))
            pad_rows = lambda w2d, i: jnp.pad(w2d, ((0, RW_LOW[i] - w2d.shape[0]), (0, 0)))
            v2 = bf(pad_rows(rw_v2[l - 1], 3)) if l > 0 else None
            y_rw, v_first = rwkv_time_mix(p_rkv, p_low, v_first, vecs, bf(pad_rows(rw_w2[l], 0)),
                                          bf(pad_rows(rw_a2[l], 1)), bf(pad_rows(rw_g2[l], 2)), v2,
                                          rw_ln_g[l], rw_ln_b[l])

            y = merge(ha, y_sb, y_gla, y_rw, bf(gate_c[l]), gate_b[l], bf(branch_w[l]), bf(w_out[l]))
            xs, h = resid_norm(xs, y, mix_post[l], xa_pre[l], 1.0)

            mem_n = rmsnorm(mem_b, mem_norm[l])
            kx = mm(mem_n, bf(xa_wk[l]), BF16)
            vx = mm(mem_n, bf(xa_wv[l]), BF16)
            xs, h = xattn_resid(h, xs, bf(xa_wq[l]), kx, vx, bf(xa_wo[l]), xa_post[l], ffn2_pre[l])

            u = ffn_up(h, ffn_w[3], ffn_w[4], l)
            g_next = ffn1_pre[l + 1] if l + 1 < depth else ffn1_pre[0]
            xs, h = mm_resid(u, ffn_w[5], l, xs, ffn2_post[l], g_next, 0.5)
        outs.append(xs)
    return jnp.stack(outs, axis=0)
```

```python
import functools
import math

import jax
import jax.numpy as jnp
from jax import lax
from jax.experimental import pallas as pl
from jax.experimental.pallas import tpu as pltpu

F32 = jnp.float32
BF16 = jnp.bfloat16

EPS = 1e-6
CHUNK = 64
SB_HEADS, SB_DH, SB_W = 4, 256, 1024
GLA_HEADS, GLA_DK, GLA_DV, GLA_KW, GLA_VW = 4, 128, 256, 512, 1024
GLA_TAU = 16.0
RW_HEADS, RW_DH, RW_W = 16, 64, 1024
RW_LN_EPS = 64e-5
RW_CHUNK = 16
XA_HEADS, XA_DH, XA_W = 4, 256, 1024
LANES = 128
VMEM_LIMIT = 56 * 1024 * 1024


def _params(*sem):
    return pltpu.CompilerParams(dimension_semantics=sem, vmem_limit_bytes=VMEM_LIMIT)


def _dot(a, b):
    return jnp.dot(a, b, preferred_element_type=F32)


def _dot_nt(a, b):
    return lax.dot_general(a, b, (((1,), (1,)), ((), ())), preferred_element_type=F32)


def _dot_tn(a, b):
    return lax.dot_general(a, b, (((0,), (0,)), ((), ())), preferred_element_type=F32)


def _split(x):
    hi = x.astype(BF16)
    lo = (x - hi.astype(F32)).astype(BF16)
    return hi, lo


def _softplus(z):
    return jnp.maximum(z, 0.0) + jnp.log1p(jnp.exp(-jnp.abs(z)))


def _sigmoid(z):
    return 1.0 / (1.0 + jnp.exp(-z))


def _tile(n, pref):
    t = min(n, pref)
    while n % t:
        t -= LANES
    return t


def _mm_kernel(a_ref, b_ref, o_ref):
    o_ref[...] = _dot(a_ref[...].astype(BF16), b_ref[...]).astype(o_ref.dtype)


def _mm_scaled_kernel(a_ref, b_ref, s_ref, o_ref):
    o_ref[...] = (_dot(a_ref[...].astype(BF16), b_ref[...]) * s_ref[...]).astype(o_ref.dtype)


def mm(a, b, out_dtype=F32, tm=1024, tn=1024, layer=None, col0=0, ncols=None, col_scale=None):
    M, K = a.shape
    N = b.shape[-1] - col0 if ncols is None else ncols
    tm = min(tm, M)
    tn = _tile(math.gcd(N, col0) if col0 else N, tn)
    assert M % tm == 0 and N % tn == 0 and col0 % tn == 0
    c0 = col0 // tn
    if layer is None:
        b_spec = pl.BlockSpec((K, tn), lambda i, j: (0, c0 + j))
    else:
        b_spec = pl.BlockSpec((None, K, tn), lambda i, j: (layer, 0, c0 + j))
    in_specs = [pl.BlockSpec((tm, K), lambda i, j: (i, 0)), b_spec]
    args = [a, b]
    if col_scale is not None:
        in_specs.append(pl.BlockSpec((1, tn), lambda i, j: (0, j)))
        args.append(col_scale.reshape(1, N).astype(F32))
    return pl.pallas_call(
        _mm_kernel if col_scale is None else _mm_scaled_kernel,
        grid=(M // tm, N // tn),
        in_specs=in_specs,
        out_specs=pl.BlockSpec((tm, tn), lambda i, j: (i, j)),
        out_shape=jax.ShapeDtypeStruct((M, N), out_dtype),
        compiler_params=_params("parallel", "parallel"),
        name="mm",
    )(*args)


def _rms(x, g):
    return x * lax.rsqrt(jnp.mean(x * x, axis=-1, keepdims=True) + EPS) * g


def _rmsnorm_kernel(x_ref, g_ref, o_ref):
    o_ref[...] = _rms(x_ref[...], g_ref[...]).astype(o_ref.dtype)


def rmsnorm(x, g, out_dtype=BF16, tm=256):
    M, D = x.shape
    tm = min(tm, M)
    return pl.pallas_call(
        _rmsnorm_kernel,
        grid=(M // tm,),
        in_specs=[pl.BlockSpec((tm, D), lambda i: (i, 0)),
                  pl.BlockSpec((1, D), lambda i: (0, 0))],
        out_specs=pl.BlockSpec((tm, D), lambda i: (i, 0)),
        out_shape=jax.ShapeDtypeStruct((M, D), out_dtype),
        compiler_params=_params("parallel"),
        name="rmsnorm",
    )(x, g.reshape(1, D))


def _resid_norm_kernel(x_ref, y_ref, gp_ref, gn_ref, xo_ref, h_ref, *, coef):
    xn = x_ref[...] + coef * _rms(y_ref[...], gp_ref[...])
    xo_ref[...] = xn
    h_ref[...] = _rms(xn, gn_ref[...]).astype(h_ref.dtype)


def resid_norm(x, y, g_post, g_next, coef, tm=256):
    M, D = x.shape
    tm = min(tm, M)
    row = pl.BlockSpec((tm, D), lambda i: (i, 0))
    vec = pl.BlockSpec((1, D), lambda i: (0, 0))
    return pl.pallas_call(
        functools.partial(_resid_norm_kernel, coef=coef),
        grid=(M // tm,),
        in_specs=[row, row, vec, vec],
        out_specs=[row, row],
        out_shape=[jax.ShapeDtypeStruct((M, D), F32), jax.ShapeDtypeStruct((M, D), BF16)],
        compiler_params=_params("parallel"),
        name="resid_norm",
    )(x, y, g_post.reshape(1, D), g_next.reshape(1, D))


def _ffn_up_kernel(h_ref, w1_ref, w3_ref, u_ref):
    h = h_ref[...]
    a = _dot(h, w1_ref[...])
    b = _dot(h, w3_ref[...])
    u_ref[...] = (a * _sigmoid(a) * b).astype(u_ref.dtype)


def ffn_up(h, w1, w3, layer, tm=1024, tf=512):
    M, D = h.shape
    Fd = w1.shape[-1]
    tm = min(tm, M)
    tf = min(tf, Fd)
    wspec = pl.BlockSpec((None, D, tf), lambda i, j: (layer, 0, j))
    return pl.pallas_call(
        _ffn_up_kernel,
        grid=(M // tm, Fd // tf),
        in_specs=[pl.BlockSpec((tm, D), lambda i, j: (i, 0)), wspec, wspec],
        out_specs=pl.BlockSpec((tm, tf), lambda i, j: (i, j)),
        out_shape=jax.ShapeDtypeStruct((M, Fd), BF16),
        compiler_params=_params("parallel", "parallel"),
        name="ffn_up",
    )(h, w1, w3)


def _mm_resid_kernel(a_ref, w_ref, x_ref, gp_ref, gn_ref, xo_ref, ho_ref, *, coef):
    y = _dot(a_ref[...], w_ref[...])
    xn = x_ref[...] + coef * _rms(y, gp_ref[...])
    xo_ref[...] = xn
    ho_ref[...] = _rms(xn, gn_ref[...]).astype(ho_ref.dtype)


def mm_resid(a, w, layer, x, g_post, g_next, coef, tm=256):
    M, K = a.shape
    D = w.shape[-1]
    tm = min(tm, M)
    row = pl.BlockSpec((tm, D), lambda i: (i, 0))
    vec = pl.BlockSpec((1, D), lambda i: (0, 0), pipeline_mode=pl.Buffered(1))
    return pl.pallas_call(
        functools.partial(_mm_resid_kernel, coef=coef),
        grid=(M // tm,),
        in_specs=[pl.BlockSpec((tm, K), lambda i: (i, 0)),
                  pl.BlockSpec((None, K, D), lambda i: (layer, 0, 0), pipeline_mode=pl.Buffered(1)),
                  row, vec, vec],
        out_specs=[row, row],
        out_shape=[jax.ShapeDtypeStruct((M, D), F32), jax.ShapeDtypeStruct((M, D), BF16)],
        compiler_params=_params("parallel"),
        name="mm_resid",
    )(a, w, x, g_post.reshape(1, D), g_next.reshape(1, D))


def _merge_kernel(ha_ref, ysb_ref, ygl_ref, yrw_ref, gc_ref, gb_ref, bw_ref, o_ref, *, rank):
    ha = ha_ref[...]
    m = None
    for g, y_ref in enumerate((ysb_ref, ygl_ref, yrw_ref)):
        gate = _sigmoid(_dot(ha[:, g * rank:(g + 1) * rank], gc_ref[g]) + gb_ref[g])
        term = gate * _dot(y_ref[...], bw_ref[g])
        m = term if m is None else m + term
    o_ref[...] = m.astype(o_ref.dtype)


def merge(ha, y_sb, y_gla, y_rw, gate_c, gate_b, branch_w, layer, tm=1024, tn=512):
    M = ha.shape[0]
    _, nb, rank, D = gate_c.shape
    W = branch_w.shape[2]
    tm = min(tm, M)
    tn = min(tn, D)
    ybs = pl.BlockSpec((tm, W), lambda i, j: (i, 0))
    return pl.pallas_call(
        functools.partial(_merge_kernel, rank=rank),
        grid=(M // tm, D // tn),
        in_specs=[pl.BlockSpec((tm, nb * rank), lambda i, j: (i, 0)), ybs, ybs, ybs,
                  pl.BlockSpec((None, nb, rank, tn), lambda i, j: (layer, 0, 0, j)),
                  pl.BlockSpec((nb, 1, tn), lambda i, j: (0, 0, j)),
                  pl.BlockSpec((None, nb, W, tn), lambda i, j: (layer, 0, 0, j))],
        out_specs=pl.BlockSpec((tm, tn), lambda i, j: (i, j)),
        out_shape=jax.ShapeDtypeStruct((M, D), BF16),
        compiler_params=_params("parallel", "parallel"),
        name="merge",
    )(ha, y_sb, y_gla, y_rw, gate_c, gate_b.reshape(nb, 1, D), branch_w)


def _xattn_kernel(h_ref, x_ref, wq_ref, k_ref, v_ref, wo_ref, gp_ref, gn_ref, xo_ref, ho_ref):
    q = _dot(h_ref[...], wq_ref[...])
    outs = []
    for hh in range(XA_HEADS):
        sl = slice(hh * XA_DH, (hh + 1) * XA_DH)
        s = _dot_nt(q[:, sl].astype(BF16), k_ref[:, sl]) * (XA_DH ** -0.5)
        e = jnp.exp(s - jnp.max(s, axis=-1, keepdims=True))
        p = e / jnp.sum(e, axis=-1, keepdims=True)
        outs.append(_dot(p.astype(BF16), v_ref[:, sl]).astype(BF16))
    y = _dot(jnp.concatenate(outs, axis=1), wo_ref[...])
    xn = x_ref[...] + _rms(y, gp_ref[...])
    xo_ref[...] = xn
    ho_ref[...] = _rms(xn, gn_ref[...]).astype(ho_ref.dtype)


def xattn_resid(h, x, wq, k, v, wo, g_post, g_next, tm=256):
    M, D = h.shape
    tm = min(tm, M)
    row = pl.BlockSpec((tm, D), lambda i: (i, 0))
    whole = lambda arr: pl.BlockSpec(arr.shape, lambda i: (0, 0), pipeline_mode=pl.Buffered(1))
    gp, gn = g_post.reshape(1, D), g_next.reshape(1, D)
    return pl.pallas_call(
        _xattn_kernel,
        grid=(M // tm,),
        in_specs=[row, row, whole(wq), whole(k), whole(v), whole(wo), whole(gp), whole(gn)],
        out_specs=[row, row],
        out_shape=[jax.ShapeDtypeStruct((M, D), F32), jax.ShapeDtypeStruct((M, D), BF16)],
        compiler_params=_params("parallel"),
        name="xattn_resid",
    )(h, x, wq, k, v, wo, gp, gn)


SB_SUB = 256
SB_ROWS = 256
SB_QSCALE = SB_DH ** -0.5 * math.log2(math.e)


def _sb_kernel(qi_tab, kj_tab, q_ref, k_ref, v_ref, o_ref, acc_ref, carry_ref, *, tb):
    step = pl.program_id(1)
    qi = qi_tab[step]
    kj = kj_tab[step]
    sub = min(SB_SUB, tb)
    nsub = tb // sub

    @pl.when(kj == qi)
    def _():
        acc_ref[...] = jnp.zeros_like(acc_ref)
        carry_ref[...] = jnp.zeros_like(carry_ref)

    def body(diag):
        jj = lax.broadcasted_iota(jnp.int32, (sub, sub), 0)
        ss = lax.broadcasted_iota(jnp.int32, (sub, sub), 1)
        neg_tri = jnp.where(jj > ss, -1.0, 0.0).astype(BF16)
        rb_rows = min(SB_ROWS, tb)

        def row_block(rb):
            r0 = rb * rb_rows
            rs = slice(r0, r0 + rb_rows)
            z_rb = _dot_nt(q_ref[rs, :], k_ref[...])
            yield
            carry = carry_ref[rs, :]
            a_parts = [None] * nsub
            for c in reversed(range(nsub)):
                c0 = c * sub
                if diag and r0 + rb_rows - 1 <= c0:
                    a_parts[c] = jnp.zeros((rb_rows, sub), BF16)
                    continue
                masked = diag and r0 <= c0 + sub - 1
                z = z_rb[:, c0:c0 + sub]
                neg_abs = lax.bitcast_convert_type(
                    lax.bitcast_convert_type(z, jnp.uint32) | jnp.uint32(0x80000000), F32)
                sp = jnp.maximum(z, 0.0) + jnp.log2(1.0 + jnp.exp2(neg_abs))
                if masked:
                    row = lax.broadcasted_iota(jnp.int32, z.shape, 0) + r0
                    col = lax.broadcasted_iota(jnp.int32, z.shape, 1) + c0
                    causal = col < row
                    sp = jnp.where(causal, sp, 0.0)
                spb = sp.astype(BF16)
                yield
                later = _dot(spb, neg_tri) + carry
                yield
                a = jnp.exp2(z - sp + later)
                if masked:
                    a = jnp.where(causal, a, 0.0)
                a_parts[c] = a.astype(BF16)
                carry = later[:, 0:1] - sp[:, 0:1]
                yield
            carry_ref[rs, :] = carry
            acc_ref[rs, :] += _dot(jnp.concatenate(a_parts, axis=1), v_ref[...])

        nrb = tb // rb_rows
        gens = [row_block(rb) for rb in range(nrb)]
        live = []
        pending = list(range(nrb))
        while pending or live:
            if pending:
                live.append(gens[pending.pop(0)])
            for g in list(live):
                try:
                    next(g)
                except StopIteration:
                    live.remove(g)

    @pl.when(kj == qi)
    def _():
        body(True)

    @pl.when(kj != qi)
    def _():
        body(False)

    @pl.when(kj == 0)
    def _():
        o_ref[...] = acc_ref[...].astype(o_ref.dtype)


def sb_attention(p_sb, tb=2048):
    T = p_sb.shape[0]
    tb = min(tb, T)
    nq = T // tb
    qi_l, kj_l = [], []
    for i in range(nq):
        for j in range(i, -1, -1):
            qi_l.append(i)
            kj_l.append(j)
    qi_tab = jnp.asarray(qi_l, jnp.int32)
    kj_tab = jnp.asarray(kj_l, jnp.int32)
    H = SB_HEADS
    grid_spec = pltpu.PrefetchScalarGridSpec(
        num_scalar_prefetch=2,
        grid=(H, len(qi_l)),
        in_specs=[pl.BlockSpec((tb, SB_DH), lambda h, s, qt, kt: (qt[s], h)),
                  pl.BlockSpec((tb, SB_DH), lambda h, s, qt, kt: (kt[s], H + h)),
                  pl.BlockSpec((tb, SB_DH), lambda h, s, qt, kt: (kt[s], 2 * H + h))],
        out_specs=pl.BlockSpec((tb, SB_DH), lambda h, s, qt, kt: (qt[s], h)),
        scratch_shapes=[pltpu.VMEM((tb, SB_DH), F32), pltpu.VMEM((tb, 1), F32)],
    )
    return pl.pallas_call(
        functools.partial(_sb_kernel, tb=tb),
        grid_spec=grid_spec,
        out_shape=jax.ShapeDtypeStruct((T, SB_W), BF16),
        compiler_params=_params("parallel", "arbitrary"),
        name="sb_attention",
    )(qi_tab, kj_tab, p_sb, p_sb, p_sb)


def _gla_kernel(q_ref, k_ref, v_ref, g_ref, ap_ref, a2_ref, ab_ref, ng_ref, o_ref, s_ref, *, tb):
    C = CHUNK
    nch = tb // C

    @pl.when(pl.program_id(0) == 0)
    def _():
        s_ref[...] = jnp.zeros_like(s_ref)

    rr = lax.broadcasted_iota(jnp.int32, (tb, tb), 0)
    cc = lax.broadcasted_iota(jnp.int32, (tb, tb), 1)
    same = (rr // C) == (cc // C)
    cum_incl = jnp.where(same & (cc <= rr), 1.0, 0.0).astype(BF16)
    cum_all = jnp.where(same, 1.0, 0.0).astype(BF16)
    pre = _dot(ap_ref[...].astype(BF16), a2_ref[...]) + ab_ref[...]
    la = -_softplus(-pre) * (1.0 / GLA_TAU)
    hi, lo = _split(la)
    b = _dot(cum_incl, hi) + _dot(cum_incl, lo)
    b_last = _dot(cum_all, hi) + _dot(cum_all, lo)
    k = k_ref[...]
    q_in = (q_ref[...] * (GLA_DK ** -0.5) * jnp.exp(b)).astype(BF16)
    k_in = (k * jnp.exp(-b)).astype(BF16)
    k_end = (k * jnp.exp(b_last - b)).astype(BF16)
    dec = jnp.exp(b_last)

    tt = lax.broadcasted_iota(jnp.int32, (C, C), 0)
    ss = lax.broadcasted_iota(jnp.int32, (C, C), 1)
    causal = ss <= tt

    def head(hh):
        kc = slice(hh * GLA_DK, (hh + 1) * GLA_DK)
        vc = slice(hh * GLA_DV, (hh + 1) * GLA_DV)
        st = s_ref[hh]
        for c in range(nch):
            sl = slice(c * C, (c + 1) * C)
            qi = q_in[sl, kc]
            v = v_ref[sl, vc].astype(BF16)
            scores = jnp.where(causal, _dot_nt(qi, k_in[sl, kc]), 0.0)
            kv = _dot_tn(v, k_end[sl, kc])
            yield
            o = _dot(scores.astype(BF16), v) + _dot_nt(qi, st.astype(BF16))
            st = dec[c * C:c * C + 1, kc] * st + kv
            yield
            o = o * lax.rsqrt(jnp.mean(o * o, axis=-1, keepdims=True) + EPS) * ng_ref[...]
            g = g_ref[sl, vc]
            o_ref[sl, vc] = (o * (g * _sigmoid(g))).astype(o_ref.dtype)
        s_ref[hh] = st

    live = [head(hh) for hh in range(GLA_HEADS)]
    while live:
        for gen in list(live):
            try:
                next(gen)
            except StopIteration:
                live.remove(gen)


def gla(p_gla, p_ap, ap_blk, a2p, ab, norm_g, tb=512):
    T = p_gla.shape[0]
    tb = min(tb, T)
    assert 2 * GLA_KW == GLA_VW
    return pl.pallas_call(
        functools.partial(_gla_kernel, tb=tb),
        grid=(T // tb,),
        in_specs=[pl.BlockSpec((tb, GLA_KW), lambda i: (i, 0)),
                  pl.BlockSpec((tb, GLA_KW), lambda i: (i, 1)),
                  pl.BlockSpec((tb, GLA_VW), lambda i: (i, 1)),
                  pl.BlockSpec((tb, GLA_VW), lambda i: (i, 2)),
                  pl.BlockSpec((tb, LANES), lambda i: (i, ap_blk)),
                  pl.BlockSpec((LANES, GLA_KW), lambda i: (0, 0)),
                  pl.BlockSpec((1, GLA_KW), lambda i: (0, 0)),
                  pl.BlockSpec((1, GLA_DV), lambda i: (0, 0))],
        out_specs=pl.BlockSpec((tb, GLA_VW), lambda i: (i, 0)),
        out_shape=jax.ShapeDtypeStruct((T, GLA_VW), BF16),
        scratch_shapes=[pltpu.VMEM((GLA_HEADS, GLA_DV, GLA_DK), F32)],
        compiler_params=_params("arbitrary"),
        name="gla",
    )(p_gla, p_gla, p_gla, p_gla, p_ap, a2p, ab.reshape(1, GLA_KW), norm_g.reshape(1, GLA_DV))


def _wkv_kernel(rd_ref, ad_ref, bh_ref, kh_ref, be_ref, ke_ref, v_ref, dec_ref, o_ref,
                s_ref, rhs_s, uvt_s, pc_s, bc_s, mrb_s, oloc_s, *, R, NS):
    C = RW_CHUNK
    N = RW_DH
    HB = RW_HEADS
    nch = R // C
    j = pl.program_id(0)

    @pl.when(j == 0)
    def _():
        for ref in (s_ref, rhs_s, uvt_s, pc_s, bc_s, mrb_s, oloc_s):
            ref[...] = jnp.zeros_like(ref)

    w_base = (j % 2) * (NS * HB)
    r_base = (1 - j % 2) * (NS * HB)

    row = lax.broadcasted_iota(jnp.int32, (R, R), 0)
    col = lax.broadcasted_iota(jnp.int32, (R, R), 1)
    same = (row // C) == (col // C)
    m_incl = same & (col <= row)
    m_strict = same & (col < row)
    m_incl_t = same & (row <= col)
    one = lambda m: jnp.where(m, 1.0, 0.0).astype(BF16)
    eye_r = one(row == col)
    eye_rf = jnp.where(row == col, 1.0, 0.0)
    ri = lax.broadcasted_iota(jnp.int32, (N, N), 0)
    ci = lax.broadcasted_iota(jnp.int32, (N, N), 1)
    eye_n = one(ri == ci)
    lane_chunk = lax.broadcasted_iota(jnp.int32, (N, R), 1) // C
    steps = C.bit_length() - 2

    def prepare(ti, hd):
        rows = slice(ti * R, (ti + 1) * R)
        lanes = slice(hd * N, (hd + 1) * N)
        r_d, a_d, b_h, k_h, b_e, k_e, v = (ref[rows, lanes] for ref in
                                           (rd_ref, ad_ref, bh_ref, kh_ref, be_ref, ke_ref, v_ref))
        gram = _dot_nt(jnp.concatenate([a_d, r_d], axis=0), jnp.concatenate([b_h, k_h], axis=0))
        l_ab = jnp.where(m_strict, gram[:R, :R], 0.0)
        l_ak = jnp.where(m_strict, gram[:R, R:], 0.0).astype(BF16)
        m_rk = jnp.where(m_incl, gram[R:, R:], 0.0).astype(BF16)
        m_rb_t = jnp.where(m_incl_t, _dot_nt(b_h, r_d), 0.0).astype(BF16)
        xt = _dot_nt(eye_n, jnp.concatenate([v, a_d, r_d], axis=0))
        v_t = xt[:, :R].astype(BF16)
        a_dt = xt[:, R:2 * R].astype(BF16)
        r_dt = xt[:, 2 * R:].astype(BF16)
        yield

        inv_m = eye_rf + l_ab
        p = l_ab
        lv_t = _dot_nt(v_t, l_ak).astype(BF16)
        o_loct = _dot_nt(v_t, m_rk)
        for _ in range(steps):
            pb = p.astype(BF16)
            p = _dot(pb, pb)
            yield
            inv_m = inv_m + _dot(inv_m.astype(BF16), p.astype(BF16))
            yield
        inv_b = inv_m.astype(BF16)
        w_at = _dot_nt(a_dt, inv_b).astype(BF16)
        u_vt = _dot_nt(lv_t, inv_b)
        yield
        stack = lambda x: jnp.concatenate(
            [jnp.where(lane_chunk == c, x, jnp.zeros_like(x)) for c in range(nch)], axis=0)
        p_c = _dot(stack(w_at), b_e).astype(BF16)
        b_c = _dot(jnp.concatenate([stack(u_vt.astype(BF16)), stack(v_t)], axis=1),
                   jnp.concatenate([b_e, k_e], axis=0))
        iw = w_base + ti * HB + hd
        rhs_s[iw] = jnp.concatenate([w_at, r_dt], axis=1)
        uvt_s[iw] = u_vt
        pc_s[iw] = p_c
        bc_s[iw] = b_c
        mrb_s[iw] = m_rb_t
        oloc_s[iw] = o_loct

    st = [s_ref[hd] for hd in range(HB)]

    def scan(ti):
        entering = [[] for _ in range(HB)]
        for c in range(nch):
            for hd in range(HB):
                ir = r_base + ti * HB + hd
                sb = st[hd].astype(BF16)
                entering[hd].append(sb)
                dec = dec_ref[ti * R + c * C:ti * R + c * C + 1, hd * N:(hd + 1) * N]
                st[hd] = dec * st[hd] + _dot(sb, pc_s[ir, c * N:(c + 1) * N, :]) + bc_s[ir, c * N:(c + 1) * N, :]
            yield

        def outputs(hd):
            ir = r_base + ti * HB + hd
            res = _dot(jnp.concatenate(entering[hd], axis=0), rhs_s[ir])
            yield
            u_t = uvt_s[ir]
            o_t = oloc_s[ir]
            for c in range(nch):
                m = lane_chunk == c
                u_t = u_t + jnp.where(m, res[c * N:(c + 1) * N, :R], 0.0)
                o_t = o_t + jnp.where(m, res[c * N:(c + 1) * N, R:], 0.0)
            o_t = o_t + _dot(u_t.astype(BF16), mrb_s[ir])
            yield
            hi, lo = _split(o_t)
            o_ref[ti * R:(ti + 1) * R, hd * N:(hd + 1) * N] = _dot_nt(eye_r, hi) + _dot_nt(eye_r, lo)

        outs = [outputs(hd) for hd in range(HB)]
        while outs:
            for gen in list(outs):
                try:
                    next(gen)
                except StopIteration:
                    outs.remove(gen)
            yield

    def scans():
        for ti in range(NS):
            yield from scan(ti)

    live = [scans()] + [prepare(ti, hd) for ti in range(NS) for hd in range(HB)]
    while live:
        for gen in list(live):
            try:
                next(gen)
            except StopIteration:
                live.remove(gen)
    for hd in range(HB):
        s_ref[hd] = st[hd]


def wkv7(rd, ad, bh, kh, be, ke, vb, dec, R=128, NS=1):
    T = rd.shape[0]
    R = min(R, T)
    NS = min(NS, T // R)
    nblk = T // (NS * R)
    N, H = RW_DH, RW_HEADS
    nbuf = 2 * NS * H
    prep_spec = pl.BlockSpec((NS * R, RW_W), lambda j: (jnp.minimum(j, nblk - 1), 0))
    scan_spec = pl.BlockSpec((NS * R, RW_W), lambda j: (jnp.maximum(j - 1, 0), 0))
    return pl.pallas_call(
        functools.partial(_wkv_kernel, R=R, NS=NS),
        grid=(nblk + 1,),
        in_specs=[prep_spec] * 7 + [scan_spec],
        out_specs=scan_spec,
        out_shape=jax.ShapeDtypeStruct((T, RW_W), F32),
        scratch_shapes=[pltpu.VMEM((H, N, N), F32),
                        pltpu.VMEM((nbuf, N, 2 * R), BF16), pltpu.VMEM((nbuf, N, R), F32),
                        pltpu.VMEM((nbuf, (R // RW_CHUNK) * N, N), BF16),
                        pltpu.VMEM((nbuf, (R // RW_CHUNK) * N, N), F32),
                        pltpu.VMEM((nbuf, R, R), BF16), pltpu.VMEM((nbuf, N, R), F32)],
        compiler_params=_params("arbitrary"),
        name="wkv7",
    )(rd, ad, bh, kh, be, ke, vb, dec)


RW_LOW = (128, 128, 512, 128)
RW_VEC_ROWS = 16
HALO = 8


def _head_sum(x, bd):
    m = x.shape[0]
    hi = x.astype(BF16)
    r1 = x - hi.astype(F32)
    mid = r1.astype(BF16)
    lo = (r1 - mid.astype(F32)).astype(BF16)
    pieces = jnp.concatenate([hi, mid, lo], axis=0)
    outs = []
    for j in range(RW_W // LANES):
        s = _dot(pieces[:, j * LANES:(j + 1) * LANES], bd)
        outs.append(s[:m] + s[m:2 * m] + s[2 * m:])
    return jnp.concatenate(outs, axis=1)


def _head_ones():
    ri = lax.broadcasted_iota(jnp.int32, (LANES, LANES), 0)
    ci = lax.broadcasted_iota(jnp.int32, (LANES, LANES), 1)
    return jnp.where((ri // RW_DH) == (ci // RW_DH), 1.0, 0.0).astype(BF16)


def _rwkv_prep_kernel(*refs, tm, first):
    if first:
        (p_ref, halo_ref, pl_ref, halo_l_ref, vec_ref, w2_ref, a2_ref, g2_ref,
         rd_ref, ad_ref, bh_ref, kh_ref, be_ref, ke_ref, vb_ref, dec_ref, g_ref, bonus_ref, vf_out_ref) = refs
    else:
        (p_ref, halo_ref, pl_ref, halo_l_ref, vf_ref, vec_ref, w2_ref, a2_ref, g2_ref, v2_ref,
         rd_ref, ad_ref, bh_ref, kh_ref, be_ref, ke_ref, vb_ref, dec_ref, g_ref, bonus_ref) = refs
    C = RW_CHUNK
    W = RW_W
    slots = RW_LOW[:3] if first else RW_LOW
    nlr = sum(slots)
    has_prev = (pl.program_id(0) > 0).astype(F32)
    row1 = lax.broadcasted_iota(jnp.int32, (tm, 1), 0)

    def shifted(ref, h_ref, c0, c1):
        prev_last = h_ref[HALO - 1:HALO, c0:c1] * has_prev
        return jnp.where(row1 == 0, prev_last, pltpu.roll(ref[:, c0:c1], 1, axis=0))

    mu_r, mu_k, mu_v, w0, a0, k_k, k_a, r_k, v0 = (vec_ref[i:i + 1, :] for i in range(9))
    r_p, k_p, v_p = (p_ref[:, i * W:(i + 1) * W] for i in range(3))
    r = r_p + (shifted(p_ref, halo_ref, 0, W) - r_p) * mu_r
    k = k_p + (shifted(p_ref, halo_ref, W, 2 * W) - k_p) * mu_k
    v = v_p + (shifted(p_ref, halo_ref, 2 * W, 3 * W) - v_p) * mu_v
    low = pl_ref[:, :nlr] + shifted(pl_ref, halo_l_ref, nlr, 2 * nlr)
    o_w, o_a, o_g, o_v = 0, slots[0], slots[0] + slots[1], slots[0] + slots[1] + slots[2]
    w_log = -_softplus(-(w0 + _dot(jnp.tanh(low[:, o_w:o_a]).astype(BF16), w2_ref[...]))) - 0.5
    lw = -jnp.exp(w_log)
    a = _sigmoid(a0 + _dot(low[:, o_a:o_g].astype(BF16), a2_ref[...]))
    g_ref[...] = _dot(_sigmoid(low[:, o_g:o_v]).astype(BF16), g2_ref[...])
    if first:
        vf_out_ref[...] = v
    else:
        v = v + (vf_ref[...] - v) * _sigmoid(v0 + _dot(low[:, o_v:nlr].astype(BF16), v2_ref[...]))

    bd = _head_ones()
    kk = k * k_k
    kk = kk * lax.rsqrt(jnp.maximum(_head_sum(kk * kk, bd), 1e-24))
    k = k * (1.0 + (a - 1.0) * k_a)
    bonus_ref[...] = _head_sum(r * k * r_k, bd) * v
    b_vec = kk * a

    rr = lax.broadcasted_iota(jnp.int32, (tm, tm), 0)
    cc = lax.broadcasted_iota(jnp.int32, (tm, tm), 1)
    same = (rr // C) == (cc // C)
    cum_incl = jnp.where(same & (cc <= rr), 1.0, 0.0).astype(BF16)
    cum_all = jnp.where(same, 1.0, 0.0).astype(BF16)
    lw_hi, lw_lo = _split(lw)
    g_in = _dot(cum_incl, lw_hi) + _dot(cum_incl, lw_lo)
    g_last = _dot(cum_all, lw_hi) + _dot(cum_all, lw_lo)
    rd_ref[...] = (r * jnp.exp(g_in)).astype(BF16)
    ad_ref[...] = (-kk * jnp.exp(g_in - lw)).astype(BF16)
    inv = jnp.exp(-g_in)
    bh_ref[...] = (b_vec * inv).astype(BF16)
    kh_ref[...] = (k * inv).astype(BF16)
    to_end = jnp.exp(g_last - g_in)
    be_ref[...] = (b_vec * to_end).astype(BF16)
    ke_ref[...] = (k * to_end).astype(BF16)
    vb_ref[...] = v.astype(BF16)
    dec_ref[...] = jnp.exp(g_last)


def rwkv_prep(p_rkv, p_low, v_first, vecs, w2, a2, g2, v2, tm=256):
    T = p_rkv.shape[0]
    tm = min(tm, T)
    first = v_first is None
    row = lambda n: pl.BlockSpec((tm, n), lambda i: (i, 0))
    full = lambda arr: pl.BlockSpec(arr.shape, lambda i: (0, 0))
    halo = lambda n: pl.BlockSpec((HALO, n), lambda i: (jnp.maximum(i * (tm // HALO) - 1, 0), 0))
    w_rkv, w_low = p_rkv.shape[1], p_low.shape[1]
    ins = ([p_rkv, p_rkv, p_low, p_low] + ([] if first else [v_first]) + [vecs, w2, a2, g2]
           + ([] if first else [v2]))
    in_specs = ([row(w_rkv), halo(w_rkv), row(w_low), halo(w_low)] + ([] if first else [row(RW_W)])
                + [full(vecs), full(w2), full(a2), full(g2)] + ([] if first else [full(v2)]))
    n_bf, n_f32 = 7, (4 if first else 3)
    outs = pl.pallas_call(
        functools.partial(_rwkv_prep_kernel, tm=tm, first=first),
        grid=(T // tm,),
        in_specs=in_specs,
        out_specs=[row(RW_W)] * (n_bf + n_f32),
        out_shape=[jax.ShapeDtypeStruct((T, RW_W), BF16)] * n_bf + [jax.ShapeDtypeStruct((T, RW_W), F32)] * n_f32,
        compiler_params=_params("parallel"),
        name="rwkv_prep",
    )(*ins)
    return outs


def _rwkv_post_kernel(o_ref, g_ref, bonus_ref, lng_ref, lnb_ref, y_ref):
    bd = _head_ones()
    o = o_ref[...]
    d = o - _head_sum(o, bd) * (1.0 / RW_DH)
    var = _head_sum(d * d, bd) * (1.0 / RW_DH)
    y = d * lax.rsqrt(var + RW_LN_EPS) * lng_ref[...] + lnb_ref[...] + bonus_ref[...]
    y_ref[...] = (y * g_ref[...]).astype(y_ref.dtype)


def rwkv_post(o, g, bonus, ln_g, ln_b, tm=256):
    T = o.shape[0]
    tm = min(tm, T)
    row = pl.BlockSpec((tm, RW_W), lambda i: (i, 0))
    vec = pl.BlockSpec((1, RW_W), lambda i: (0, 0))
    return pl.pallas_call(
        _rwkv_post_kernel,
        grid=(T // tm,),
        in_specs=[row, row, row, vec, vec],
        out_specs=row,
        out_shape=jax.ShapeDtypeStruct((T, RW_W), BF16),
        compiler_params=_params("parallel"),
        name="rwkv_post",
    )(o, g, bonus, ln_g.reshape(1, RW_W), ln_b.reshape(1, RW_W))


def rwkv_time_mix(p_rkv, p_low, v_first, vecs, w2, a2, g2, v2, ln_g, ln_b):
    outs = rwkv_prep(p_rkv, p_low, v_first, vecs, w2, a2, g2, v2)
    rd, ad, bh, kh, be, ke, vb, dec, g, bonus = outs[:10]
    if v_first is None:
        v_first = outs[10]
    o = wkv7(rd, ad, bh, kh, be, ke, vb, dec)
    return rwkv_post(o, g, bonus, ln_g, ln_b), v_first


def _pad_cols(w, mult):
    n = w.shape[1]
    pad = (-n) % mult
    return w if pad == 0 else jnp.pad(w, ((0, 0), (0, pad)))


def kernel(x, mem, ffn1_pre, ffn1_post, ffn1_w1, ffn1_w3, ffn1_w2, mix_pre, mix_post, w_in, gla_a1, gla_a2, gla_ab, gla_norm, rw_mu_rkv, rw_mu_wag, rw_w0, rw_w1, rw_w2, rw_a0, rw_a1, rw_a2, rw_g1, rw_g2, rw_k_k, rw_k_a, rw_r_k, rw_ln_g, rw_ln_b, rw_mu_vl, rw_v0, rw_v1, rw_v2, branch_w, gate_a, gate_c, gate_b, w_out, xa_pre, xa_post, mem_norm, xa_wq, xa_wk, xa_wv, xa_wo, ffn2_pre, ffn2_post, ffn2_w1, ffn2_w3, ffn2_w2):
    B, T, D = x.shape
    depth = ffn1_pre.shape[0]
    bf = lambda t: t.astype(BF16)
    sb_end = 3 * SB_W
    gla_end = sb_end + 2 * GLA_KW + 2 * GLA_VW
    w_in_b = bf(w_in)
    ffn_w = [bf(t) for t in (ffn1_w1, ffn1_w3, ffn1_w2, ffn2_w1, ffn2_w3, ffn2_w2)]
    gate_c_b, branch_w_b, w_out_b = bf(gate_c), bf(branch_w), bf(w_out)
    sb_col_scale = jnp.concatenate([jnp.full((SB_W,), SB_QSCALE, F32), jnp.ones((sb_end - SB_W,), F32)])
    outs = []
    for bi in range(B):
        xs = x[bi]
        mem_b = mem[bi]
        h = rmsnorm(xs, ffn1_pre[0])
        v_first = None
        for l in range(depth):
            u = ffn_up(h, ffn_w[0], ffn_w[1], l)
            xs, h = mm_resid(u, ffn_w[2], l, xs, ffn1_post[l], mix_pre[l], 0.5)

            p_sb = mm(h, w_in_b, BF16, layer=l, col0=0, ncols=sb_end, col_scale=sb_col_scale)
            p_gla = mm(h, w_in_b, layer=l, col0=sb_end, ncols=gla_end - sb_end)
            p_rkv = mm(h, w_in_b, layer=l, col0=gla_end, ncols=3 * RW_W)
            lows = [(rw_w1[l], rw_mu_wag[l, 0]), (rw_a1[l], rw_mu_wag[l, 1]), (rw_g1[l], rw_mu_wag[l, 2])]
            if l > 0:
                lows.append((rw_v1[l - 1], rw_mu_vl[l - 1]))
            slot = lambda wl, i: jnp.pad(wl, ((0, 0), (0, RW_LOW[i] - wl.shape[1])))
            w_low = jnp.concatenate([slot(wl * (1.0 - m)[:, None], i) for i, (wl, m) in enumerate(lows)]
                                    + [slot(wl * m[:, None], i) for i, (wl, m) in enumerate(lows)], axis=1)
            gla_ap_col = w_low.shape[1]
            w_low = jnp.concatenate([w_low, _pad_cols(gla_a1[l], LANES)], axis=1)
            p_low = mm(h, bf(_pad_cols(w_low, 1024)))
            ha = mm(h, bf(jnp.concatenate([gate_a[l, g] for g in range(gate_a.shape[1])], axis=1)), BF16)

            y_sb = sb_attention(p_sb)
            a2p = jnp.pad(gla_a2[l], ((0, LANES - gla_a2.shape[1]), (0, 0)))
            y_gla = gla(p_gla, p_low, gla_ap_col // LANES, bf(a2p), gla_ab[l], gla_norm[l])
            v0 = rw_v0[l - 1] if l > 0 else jnp.zeros((RW_W,), F32)
            vec_rows = [rw_mu_rkv[l, 0], rw_mu_rkv[l, 1], rw_mu_rkv[l, 2], rw_w0[l], rw_a0[l], rw_k_k[l], rw_k_a[l],
                        rw_r_k[l].reshape(RW_W), v0]
            vecs = jnp.pad(jnp.stack(vec_rows, axis=0), ((0, RW_VEC_ROWS - len(vec_rows)), (0, 0)))
            pad_rows = lambda w2d, i: jnp.pad(w2d, ((0, RW_LOW[i] - w2d.shape[0]), (0, 0)))
            v2 = bf(pad_rows(rw_v2[l - 1], 3)) if l > 0 else None
            y_rw, v_first = rwkv_time_mix(p_rkv, p_low, v_first, vecs, bf(pad_rows(rw_w2[l], 0)),
                                          bf(pad_rows(rw_a2[l], 1)), bf(pad_rows(rw_g2[l], 2)), v2,
                                          rw_ln_g[l], rw_ln_b[l])

            merged = merge(ha, y_sb, y_gla, y_rw, gate_c_b, gate_b[l], branch_w_b, l)
            y = mm(merged, w_out_b, layer=l)
            xs, h = resid_norm(xs, y, mix_post[l], xa_pre[l], 1.0)

            mem_n = rmsnorm(mem_b, mem_norm[l])
            kx = mm(mem_n, bf(xa_wk[l]), BF16)
            vx = mm(mem_n, bf(xa_wv[l]), BF16)
            xs, h = xattn_resid(h, xs, bf(xa_wq[l]), kx, vx, bf(xa_wo[l]), xa_post[l], ffn2_pre[l])

            u = ffn_up(h, ffn_w[3], ffn_w[4], l)
            g_next = ffn1_pre[l + 1] if l + 1 < depth else ffn1_pre[0]
            xs, h = mm_resid(u, ffn_w[5], l, xs, ffn2_post[l], g_next, 0.5)
        outs.append(xs)
    return jnp.stack(outs, axis=0)
```

```python
import functools
import math

import jax
import jax.numpy as jnp
from jax import lax
from jax.experimental import pallas as pl
from jax.experimental.pallas import tpu as pltpu

F32 = jnp.float32
BF16 = jnp.bfloat16

EPS = 1e-6
CHUNK = 64
SB_HEADS, SB_DH, SB_W = 4, 256, 1024
GLA_HEADS, GLA_DK, GLA_DV, GLA_KW, GLA_VW = 4, 128, 256, 512, 1024
GLA_TAU = 16.0
RW_HEADS, RW_DH, RW_W = 16, 64, 1024
RW_LN_EPS = 64e-5
RW_CHUNK = 16
XA_HEADS, XA_DH, XA_W = 4, 256, 1024
LANES = 128
VMEM_LIMIT = 56 * 1024 * 1024


def _params(*sem):
    return pltpu.CompilerParams(dimension_semantics=sem, vmem_limit_bytes=VMEM_LIMIT)


def _dot(a, b):
    return jnp.dot(a, b, preferred_element_type=F32)


def _dot_nt(a, b):
    return lax.dot_general(a, b, (((1,), (1,)), ((), ())), preferred_element_type=F32)


def _dot_tn(a, b):
    return lax.dot_general(a, b, (((0,), (0,)), ((), ())), preferred_element_type=F32)


def _split(x):
    hi = x.astype(BF16)
    lo = (x - hi.astype(F32)).astype(BF16)
    return hi, lo


def _softplus(z):
    return jnp.maximum(z, 0.0) + jnp.log1p(jnp.exp(-jnp.abs(z)))


def _sigmoid(z):
    return 1.0 / (1.0 + jnp.exp(-z))


def _tile(n, pref):
    t = min(n, pref)
    while n % t:
        t -= LANES
    return t


def _mm_kernel(a_ref, b_ref, o_ref):
    o_ref[...] = _dot(a_ref[...].astype(BF16), b_ref[...]).astype(o_ref.dtype)


def _mm_scaled_kernel(a_ref, b_ref, s_ref, o_ref):
    o_ref[...] = (_dot(a_ref[...].astype(BF16), b_ref[...]) * s_ref[...]).astype(o_ref.dtype)


def mm(a, b, out_dtype=F32, tm=1024, tn=1024, layer=None, col0=0, ncols=None, col_scale=None):
    M, K = a.shape
    N = b.shape[-1] - col0 if ncols is None else ncols
    tm = min(tm, M)
    tn = _tile(math.gcd(N, col0) if col0 else N, tn)
    assert M % tm == 0 and N % tn == 0 and col0 % tn == 0
    c0 = col0 // tn
    if layer is None:
        b_spec = pl.BlockSpec((K, tn), lambda i, j: (0, c0 + j))
    else:
        b_spec = pl.BlockSpec((None, K, tn), lambda i, j: (layer, 0, c0 + j))
    in_specs = [pl.BlockSpec((tm, K), lambda i, j: (i, 0)), b_spec]
    args = [a, b]
    if col_scale is not None:
        in_specs.append(pl.BlockSpec((1, tn), lambda i, j: (0, j)))
        args.append(col_scale.reshape(1, N).astype(F32))
    return pl.pallas_call(
        _mm_kernel if col_scale is None else _mm_scaled_kernel,
        grid=(M // tm, N // tn),
        in_specs=in_specs,
        out_specs=pl.BlockSpec((tm, tn), lambda i, j: (i, j)),
        out_shape=jax.ShapeDtypeStruct((M, N), out_dtype),
        compiler_params=_params("parallel", "parallel"),
        name="mm",
    )(*args)


def _rms(x, g):
    return x * lax.rsqrt(jnp.mean(x * x, axis=-1, keepdims=True) + EPS) * g


def _rmsnorm_kernel(x_ref, g_ref, o_ref):
    o_ref[...] = _rms(x_ref[...], g_ref[...]).astype(o_ref.dtype)


def rmsnorm(x, g, out_dtype=BF16, tm=256):
    M, D = x.shape
    tm = min(tm, M)
    return pl.pallas_call(
        _rmsnorm_kernel,
        grid=(M // tm,),
        in_specs=[pl.BlockSpec((tm, D), lambda i: (i, 0)),
                  pl.BlockSpec((1, D), lambda i: (0, 0))],
        out_specs=pl.BlockSpec((tm, D), lambda i: (i, 0)),
        out_shape=jax.ShapeDtypeStruct((M, D), out_dtype),
        compiler_params=_params("parallel"),
        name="rmsnorm",
    )(x, g.reshape(1, D))


def _resid_norm_kernel(x_ref, y_ref, gp_ref, gn_ref, xo_ref, h_ref, *, coef):
    xn = x_ref[...] + coef * _rms(y_ref[...], gp_ref[...])
    xo_ref[...] = xn
    h_ref[...] = _rms(xn, gn_ref[...]).astype(h_ref.dtype)


def resid_norm(x, y, g_post, g_next, coef, tm=256):
    M, D = x.shape
    tm = min(tm, M)
    row = pl.BlockSpec((tm, D), lambda i: (i, 0))
    vec = pl.BlockSpec((1, D), lambda i: (0, 0))
    return pl.pallas_call(
        functools.partial(_resid_norm_kernel, coef=coef),
        grid=(M // tm,),
        in_specs=[row, row, vec, vec],
        out_specs=[row, row],
        out_shape=[jax.ShapeDtypeStruct((M, D), F32), jax.ShapeDtypeStruct((M, D), BF16)],
        compiler_params=_params("parallel"),
        name="resid_norm",
    )(x, y, g_post.reshape(1, D), g_next.reshape(1, D))


def _ffn_up_kernel(h_ref, w1_ref, w3_ref, u_ref):
    h = h_ref[...]
    a = _dot(h, w1_ref[...])
    b = _dot(h, w3_ref[...])
    u_ref[...] = (a * _sigmoid(a) * b).astype(u_ref.dtype)


def ffn_up(h, w1, w3, layer, tm=1024, tf=512):
    M, D = h.shape
    Fd = w1.shape[-1]
    tm = min(tm, M)
    tf = min(tf, Fd)
    wspec = pl.BlockSpec((None, D, tf), lambda i, j: (layer, 0, j))
    return pl.pallas_call(
        _ffn_up_kernel,
        grid=(M // tm, Fd // tf),
        in_specs=[pl.BlockSpec((tm, D), lambda i, j: (i, 0)), wspec, wspec],
        out_specs=pl.BlockSpec((tm, tf), lambda i, j: (i, j)),
        out_shape=jax.ShapeDtypeStruct((M, Fd), BF16),
        compiler_params=_params("parallel", "parallel"),
        name="ffn_up",
    )(h, w1, w3)


def _mm_resid_kernel(a_ref, w_ref, x_ref, gp_ref, gn_ref, xo_ref, ho_ref, *, coef):
    y = _dot(a_ref[...], w_ref[...])
    xn = x_ref[...] + coef * _rms(y, gp_ref[...])
    xo_ref[...] = xn
    ho_ref[...] = _rms(xn, gn_ref[...]).astype(ho_ref.dtype)


def mm_resid(a, w, layer, x, g_post, g_next, coef, tm=256):
    M, K = a.shape
    D = w.shape[-1]
    tm = min(tm, M)
    row = pl.BlockSpec((tm, D), lambda i: (i, 0))
    vec = pl.BlockSpec((1, D), lambda i: (0, 0), pipeline_mode=pl.Buffered(1))
    return pl.pallas_call(
        functools.partial(_mm_resid_kernel, coef=coef),
        grid=(M // tm,),
        in_specs=[pl.BlockSpec((tm, K), lambda i: (i, 0)),
                  pl.BlockSpec((None, K, D), lambda i: (layer, 0, 0), pipeline_mode=pl.Buffered(1)),
                  row, vec, vec],
        out_specs=[row, row],
        out_shape=[jax.ShapeDtypeStruct((M, D), F32), jax.ShapeDtypeStruct((M, D), BF16)],
        compiler_params=_params("parallel"),
        name="mm_resid",
    )(a, w, x, g_post.reshape(1, D), g_next.reshape(1, D))


def _merge_kernel(ha_ref, ysb_ref, ygl_ref, yrw_ref, gc_ref, gb_ref, bw_ref, o_ref, *, rank):
    ha = ha_ref[...]
    m = None
    for g, y_ref in enumerate((ysb_ref, ygl_ref, yrw_ref)):
        gate = _sigmoid(_dot(ha[:, g * rank:(g + 1) * rank], gc_ref[g]) + gb_ref[g])
        term = gate * _dot(y_ref[...], bw_ref[g])
        m = term if m is None else m + term
    o_ref[...] = m.astype(o_ref.dtype)


def merge(ha, y_sb, y_gla, y_rw, gate_c, gate_b, branch_w, layer, tm=1024, tn=512):
    M = ha.shape[0]
    _, nb, rank, D = gate_c.shape
    W = branch_w.shape[2]
    tm = min(tm, M)
    tn = min(tn, D)
    ybs = pl.BlockSpec((tm, W), lambda i, j: (i, 0))
    return pl.pallas_call(
        functools.partial(_merge_kernel, rank=rank),
        grid=(M // tm, D // tn),
        in_specs=[pl.BlockSpec((tm, nb * rank), lambda i, j: (i, 0)), ybs, ybs, ybs,
                  pl.BlockSpec((None, nb, rank, tn), lambda i, j: (layer, 0, 0, j)),
                  pl.BlockSpec((nb, 1, tn), lambda i, j: (0, 0, j)),
                  pl.BlockSpec((None, nb, W, tn), lambda i, j: (layer, 0, 0, j))],
        out_specs=pl.BlockSpec((tm, tn), lambda i, j: (i, j)),
        out_shape=jax.ShapeDtypeStruct((M, D), BF16),
        compiler_params=_params("parallel", "parallel"),
        name="merge",
    )(ha, y_sb, y_gla, y_rw, gate_c, gate_b.reshape(nb, 1, D), branch_w)


def _xattn_kernel(h_ref, x_ref, wq_ref, k_ref, v_ref, wo_ref, gp_ref, gn_ref, xo_ref, ho_ref):
    q = _dot(h_ref[...], wq_ref[...])
    outs = []
    for hh in range(XA_HEADS):
        sl = slice(hh * XA_DH, (hh + 1) * XA_DH)
        s = _dot_nt(q[:, sl].astype(BF16), k_ref[:, sl]) * (XA_DH ** -0.5)
        e = jnp.exp(s - jnp.max(s, axis=-1, keepdims=True))
        p = e / jnp.sum(e, axis=-1, keepdims=True)
        outs.append(_dot(p.astype(BF16), v_ref[:, sl]).astype(BF16))
    y = _dot(jnp.concatenate(outs, axis=1), wo_ref[...])
    xn = x_ref[...] + _rms(y, gp_ref[...])
    xo_ref[...] = xn
    ho_ref[...] = _rms(xn, gn_ref[...]).astype(ho_ref.dtype)


def xattn_resid(h, x, wq, k, v, wo, layer, g_post, g_next, tm=256):
    M, D = h.shape
    tm = min(tm, M)
    row = pl.BlockSpec((tm, D), lambda i: (i, 0))
    whole = lambda arr: pl.BlockSpec(arr.shape, lambda i: (0, 0), pipeline_mode=pl.Buffered(1))
    of_layer = lambda arr: pl.BlockSpec((None,) + arr.shape[1:], lambda i: (layer, 0, 0),
                                        pipeline_mode=pl.Buffered(1))
    gp, gn = g_post.reshape(1, D), g_next.reshape(1, D)
    return pl.pallas_call(
        _xattn_kernel,
        grid=(M // tm,),
        in_specs=[row, row, of_layer(wq), whole(k), whole(v), of_layer(wo), whole(gp), whole(gn)],
        out_specs=[row, row],
        out_shape=[jax.ShapeDtypeStruct((M, D), F32), jax.ShapeDtypeStruct((M, D), BF16)],
        compiler_params=_params("parallel"),
        name="xattn_resid",
    )(h, x, wq, k, v, wo, gp, gn)


SB_SUB = 256
SB_ROWS = 256
SB_QSCALE = SB_DH ** -0.5 * math.log2(math.e)


def _sb_kernel(qi_tab, kj_tab, q_ref, k_ref, v_ref, o_ref, acc_ref, carry_ref, *, tb):
    step = pl.program_id(1)
    qi = qi_tab[step]
    kj = kj_tab[step]
    sub = min(SB_SUB, tb)
    nsub = tb // sub

    @pl.when(kj == qi)
    def _():
        acc_ref[...] = jnp.zeros_like(acc_ref)
        carry_ref[...] = jnp.zeros_like(carry_ref)

    def body(diag):
        jj = lax.broadcasted_iota(jnp.int32, (sub, sub), 0)
        ss = lax.broadcasted_iota(jnp.int32, (sub, sub), 1)
        neg_tri = jnp.where(jj > ss, -1.0, 0.0).astype(BF16)
        rb_rows = min(SB_ROWS, tb)

        def row_block(rb):
            r0 = rb * rb_rows
            rs = slice(r0, r0 + rb_rows)
            z_rb = _dot_nt(q_ref[rs, :], k_ref[...])
            yield
            carry = carry_ref[rs, :]
            a_parts = [None] * nsub
            for c in reversed(range(nsub)):
                c0 = c * sub
                if diag and r0 + rb_rows - 1 <= c0:
                    a_parts[c] = jnp.zeros((rb_rows, sub), BF16)
                    continue
                masked = diag and r0 <= c0 + sub - 1
                z = z_rb[:, c0:c0 + sub]
                neg_abs = lax.bitcast_convert_type(
                    lax.bitcast_convert_type(z, jnp.uint32) | jnp.uint32(0x80000000), F32)
                sp = jnp.maximum(z, 0.0) + jnp.log2(1.0 + jnp.exp2(neg_abs))
                if masked:
                    row = lax.broadcasted_iota(jnp.int32, z.shape, 0) + r0
                    col = lax.broadcasted_iota(jnp.int32, z.shape, 1) + c0
                    causal = col < row
                    sp = jnp.where(causal, sp, 0.0)
                spb = sp.astype(BF16)
                yield
                later = _dot(spb, neg_tri) + carry
                yield
                a = jnp.exp2(z - sp + later)
                if masked:
                    a = jnp.where(causal, a, 0.0)
                a_parts[c] = a.astype(BF16)
                carry = later[:, 0:1] - sp[:, 0:1]
                yield
            carry_ref[rs, :] = carry
            acc_ref[rs, :] += _dot(jnp.concatenate(a_parts, axis=1), v_ref[...])

        nrb = tb // rb_rows
        gens = [row_block(rb) for rb in range(nrb)]
        live = []
        pending = list(range(nrb))
        while pending or live:
            if pending:
                live.append(gens[pending.pop(0)])
            for g in list(live):
                try:
                    next(g)
                except StopIteration:
                    live.remove(g)

    @pl.when(kj == qi)
    def _():
        body(True)

    @pl.when(kj != qi)
    def _():
        body(False)

    @pl.when(kj == 0)
    def _():
        o_ref[...] = acc_ref[...].astype(o_ref.dtype)


def sb_attention(p_sb, tb=2048):
    T = p_sb.shape[0]
    tb = min(tb, T)
    nq = T // tb
    qi_l, kj_l = [], []
    for i in range(nq):
        for j in range(i, -1, -1):
            qi_l.append(i)
            kj_l.append(j)
    qi_tab = jnp.asarray(qi_l, jnp.int32)
    kj_tab = jnp.asarray(kj_l, jnp.int32)
    H = SB_HEADS
    grid_spec = pltpu.PrefetchScalarGridSpec(
        num_scalar_prefetch=2,
        grid=(H, len(qi_l)),
        in_specs=[pl.BlockSpec((tb, SB_DH), lambda h, s, qt, kt: (qt[s], h)),
                  pl.BlockSpec((tb, SB_DH), lambda h, s, qt, kt: (kt[s], H + h)),
                  pl.BlockSpec((tb, SB_DH), lambda h, s, qt, kt: (kt[s], 2 * H + h))],
        out_specs=pl.BlockSpec((tb, SB_DH), lambda h, s, qt, kt: (qt[s], h)),
        scratch_shapes=[pltpu.VMEM((tb, SB_DH), F32), pltpu.VMEM((tb, 1), F32)],
    )
    return pl.pallas_call(
        functools.partial(_sb_kernel, tb=tb),
        grid_spec=grid_spec,
        out_shape=jax.ShapeDtypeStruct((T, SB_W), BF16),
        compiler_params=_params("parallel", "arbitrary"),
        name="sb_attention",
    )(qi_tab, kj_tab, p_sb, p_sb, p_sb)


def _gla_kernel(q_ref, k_ref, v_ref, g_ref, ap_ref, a2_ref, ab_ref, ng_ref, o_ref, s_ref, *, tb):
    C = CHUNK
    nch = tb // C

    @pl.when(pl.program_id(0) == 0)
    def _():
        s_ref[...] = jnp.zeros_like(s_ref)

    rr = lax.broadcasted_iota(jnp.int32, (tb, tb), 0)
    cc = lax.broadcasted_iota(jnp.int32, (tb, tb), 1)
    same = (rr // C) == (cc // C)
    cum_incl = jnp.where(same & (cc <= rr), 1.0, 0.0).astype(BF16)
    cum_all = jnp.where(same, 1.0, 0.0).astype(BF16)
    pre = _dot(ap_ref[...].astype(BF16), a2_ref[...]) + ab_ref[...]
    la = -_softplus(-pre) * (1.0 / GLA_TAU)
    hi, lo = _split(la)
    b = _dot(cum_incl, hi) + _dot(cum_incl, lo)
    b_last = _dot(cum_all, hi) + _dot(cum_all, lo)
    k = k_ref[...]
    q_in = (q_ref[...] * (GLA_DK ** -0.5) * jnp.exp(b)).astype(BF16)
    k_in = (k * jnp.exp(-b)).astype(BF16)
    k_end = (k * jnp.exp(b_last - b)).astype(BF16)
    dec = jnp.exp(b_last)

    tt = lax.broadcasted_iota(jnp.int32, (C, C), 0)
    ss = lax.broadcasted_iota(jnp.int32, (C, C), 1)
    causal = ss <= tt

    def head(hh):
        kc = slice(hh * GLA_DK, (hh + 1) * GLA_DK)
        vc = slice(hh * GLA_DV, (hh + 1) * GLA_DV)
        st = s_ref[hh]
        for c in range(nch):
            sl = slice(c * C, (c + 1) * C)
            qi = q_in[sl, kc]
            v = v_ref[sl, vc].astype(BF16)
            scores = jnp.where(causal, _dot_nt(qi, k_in[sl, kc]), 0.0)
            kv = _dot_tn(v, k_end[sl, kc])
            yield
            o = _dot(scores.astype(BF16), v) + _dot_nt(qi, st.astype(BF16))
            st = dec[c * C:c * C + 1, kc] * st + kv
            yield
            o = o * lax.rsqrt(jnp.mean(o * o, axis=-1, keepdims=True) + EPS) * ng_ref[...]
            g = g_ref[sl, vc]
            o_ref[sl, vc] = (o * (g * _sigmoid(g))).astype(o_ref.dtype)
        s_ref[hh] = st

    live = [head(hh) for hh in range(GLA_HEADS)]
    while live:
        for gen in list(live):
            try:
                next(gen)
            except StopIteration:
                live.remove(gen)


def gla(p_gla, p_ap, ap_blk, a2p, ab, norm_g, tb=512):
    T = p_gla.shape[0]
    tb = min(tb, T)
    assert 2 * GLA_KW == GLA_VW
    return pl.pallas_call(
        functools.partial(_gla_kernel, tb=tb),
        grid=(T // tb,),
        in_specs=[pl.BlockSpec((tb, GLA_KW), lambda i: (i, 0)),
                  pl.BlockSpec((tb, GLA_KW), lambda i: (i, 1)),
                  pl.BlockSpec((tb, GLA_VW), lambda i: (i, 1)),
                  pl.BlockSpec((tb, GLA_VW), lambda i: (i, 2)),
                  pl.BlockSpec((tb, LANES), lambda i: (i, ap_blk)),
                  pl.BlockSpec((LANES, GLA_KW), lambda i: (0, 0)),
                  pl.BlockSpec((1, GLA_KW), lambda i: (0, 0)),
                  pl.BlockSpec((1, GLA_DV), lambda i: (0, 0))],
        out_specs=pl.BlockSpec((tb, GLA_VW), lambda i: (i, 0)),
        out_shape=jax.ShapeDtypeStruct((T, GLA_VW), BF16),
        scratch_shapes=[pltpu.VMEM((GLA_HEADS, GLA_DV, GLA_DK), F32)],
        compiler_params=_params("arbitrary"),
        name="gla",
    )(p_gla, p_gla, p_gla, p_gla, p_ap, a2p, ab.reshape(1, GLA_KW), norm_g.reshape(1, GLA_DV))


def _wkv_kernel(rd_ref, ad_ref, bh_ref, kh_ref, be_ref, ke_ref, v_ref, dec_ref, o_ref,
                s_ref, rhs_s, uvt_s, pc_s, bc_s, mrb_s, oloc_s, *, R, NS):
    C = RW_CHUNK
    N = RW_DH
    HB = RW_HEADS
    nch = R // C
    j = pl.program_id(0)

    @pl.when(j == 0)
    def _():
        for ref in (s_ref, rhs_s, uvt_s, pc_s, bc_s, mrb_s, oloc_s):
            ref[...] = jnp.zeros_like(ref)

    w_base = (j % 2) * (NS * HB)
    r_base = (1 - j % 2) * (NS * HB)

    row = lax.broadcasted_iota(jnp.int32, (R, R), 0)
    col = lax.broadcasted_iota(jnp.int32, (R, R), 1)
    same = (row // C) == (col // C)
    m_incl = same & (col <= row)
    m_strict = same & (col < row)
    m_incl_t = same & (row <= col)
    one = lambda m: jnp.where(m, 1.0, 0.0).astype(BF16)
    eye_r = one(row == col)
    eye_rf = jnp.where(row == col, 1.0, 0.0)
    ri = lax.broadcasted_iota(jnp.int32, (N, N), 0)
    ci = lax.broadcasted_iota(jnp.int32, (N, N), 1)
    eye_n = one(ri == ci)
    lane_chunk = lax.broadcasted_iota(jnp.int32, (N, R), 1) // C
    steps = C.bit_length() - 2

    def prepare(ti, hd):
        rows = slice(ti * R, (ti + 1) * R)
        lanes = slice(hd * N, (hd + 1) * N)
        r_d, a_d, b_h, k_h, b_e, k_e, v = (ref[rows, lanes] for ref in
                                           (rd_ref, ad_ref, bh_ref, kh_ref, be_ref, ke_ref, v_ref))
        gram = _dot_nt(jnp.concatenate([a_d, r_d], axis=0), jnp.concatenate([b_h, k_h], axis=0))
        l_ab = jnp.where(m_strict, gram[:R, :R], 0.0)
        l_ak = jnp.where(m_strict, gram[:R, R:], 0.0).astype(BF16)
        m_rk = jnp.where(m_incl, gram[R:, R:], 0.0).astype(BF16)
        m_rb_t = jnp.where(m_incl_t, _dot_nt(b_h, r_d), 0.0).astype(BF16)
        xt = _dot_nt(eye_n, jnp.concatenate([v, a_d, r_d], axis=0))
        v_t = xt[:, :R].astype(BF16)
        a_dt = xt[:, R:2 * R].astype(BF16)
        r_dt = xt[:, 2 * R:].astype(BF16)
        yield

        inv_m = eye_rf + l_ab
        p = l_ab
        lv_t = _dot_nt(v_t, l_ak).astype(BF16)
        o_loct = _dot_nt(v_t, m_rk)
        for _ in range(steps):
            pb = p.astype(BF16)
            p = _dot(pb, pb)
            yield
            inv_m = inv_m + _dot(inv_m.astype(BF16), p.astype(BF16))
            yield
        inv_b = inv_m.astype(BF16)
        w_at = _dot_nt(a_dt, inv_b).astype(BF16)
        u_vt = _dot_nt(lv_t, inv_b)
        yield
        stack = lambda x: jnp.concatenate(
            [jnp.where(lane_chunk == c, x, jnp.zeros_like(x)) for c in range(nch)], axis=0)
        p_c = _dot(stack(w_at), b_e).astype(BF16)
        b_c = _dot(jnp.concatenate([stack(u_vt.astype(BF16)), stack(v_t)], axis=1),
                   jnp.concatenate([b_e, k_e], axis=0))
        iw = w_base + ti * HB + hd
        rhs_s[iw] = jnp.concatenate([w_at, r_dt], axis=1)
        uvt_s[iw] = u_vt
        pc_s[iw] = p_c
        bc_s[iw] = b_c
        mrb_s[iw] = m_rb_t
        oloc_s[iw] = o_loct

    st = [s_ref[hd] for hd in range(HB)]

    def scan(ti):
        entering = [[] for _ in range(HB)]
        for c in range(nch):
            for hd in range(HB):
                ir = r_base + ti * HB + hd
                sb = st[hd].astype(BF16)
                entering[hd].append(sb)
                dec = dec_ref[ti * R + c * C:ti * R + c * C + 1, hd * N:(hd + 1) * N]
                st[hd] = dec * st[hd] + _dot(sb, pc_s[ir, c * N:(c + 1) * N, :]) + bc_s[ir, c * N:(c + 1) * N, :]
            yield

        def outputs(hd):
            ir = r_base + ti * HB + hd
            res = _dot(jnp.concatenate(entering[hd], axis=0), rhs_s[ir])
            yield
            u_t = uvt_s[ir]
            o_t = oloc_s[ir]
            for c in range(nch):
                m = lane_chunk == c
                u_t = u_t + jnp.where(m, res[c * N:(c + 1) * N, :R], 0.0)
                o_t = o_t + jnp.where(m, res[c * N:(c + 1) * N, R:], 0.0)
            o_t = o_t + _dot(u_t.astype(BF16), mrb_s[ir])
            yield
            hi, lo = _split(o_t)
            o_ref[ti * R:(ti + 1) * R, hd * N:(hd + 1) * N] = _dot_nt(eye_r, hi) + _dot_nt(eye_r, lo)

        outs = [outputs(hd) for hd in range(HB)]
        while outs:
            for gen in list(outs):
                try:
                    next(gen)
                except StopIteration:
                    outs.remove(gen)
            yield

    def scans():
        for ti in range(NS):
            yield from scan(ti)

    live = [scans()] + [prepare(ti, hd) for ti in range(NS) for hd in range(HB)]
    while live:
        for gen in list(live):
            try:
                next(gen)
            except StopIteration:
                live.remove(gen)
    for hd in range(HB):
        s_ref[hd] = st[hd]


def wkv7(rd, ad, bh, kh, be, ke, vb, dec, R=128, NS=1):
    T = rd.shape[0]
    R = min(R, T)
    NS = min(NS, T // R)
    nblk = T // (NS * R)
    N, H = RW_DH, RW_HEADS
    nbuf = 2 * NS * H
    prep_spec = pl.BlockSpec((NS * R, RW_W), lambda j: (jnp.minimum(j, nblk - 1), 0))
    scan_spec = pl.BlockSpec((NS * R, RW_W), lambda j: (jnp.maximum(j - 1, 0), 0))
    return pl.pallas_call(
        functools.partial(_wkv_kernel, R=R, NS=NS),
        grid=(nblk + 1,),
        in_specs=[prep_spec] * 7 + [scan_spec],
        out_specs=scan_spec,
        out_shape=jax.ShapeDtypeStruct((T, RW_W), F32),
        scratch_shapes=[pltpu.VMEM((H, N, N), F32),
                        pltpu.VMEM((nbuf, N, 2 * R), BF16), pltpu.VMEM((nbuf, N, R), F32),
                        pltpu.VMEM((nbuf, (R // RW_CHUNK) * N, N), BF16),
                        pltpu.VMEM((nbuf, (R // RW_CHUNK) * N, N), F32),
                        pltpu.VMEM((nbuf, R, R), BF16), pltpu.VMEM((nbuf, N, R), F32)],
        compiler_params=_params("arbitrary"),
        name="wkv7",
    )(rd, ad, bh, kh, be, ke, vb, dec)


RW_LOW = (128, 128, 512, 128)
RW_VEC_ROWS = 16
HALO = 8


def _head_sum(x, bd):
    m = x.shape[0]
    hi = x.astype(BF16)
    r1 = x - hi.astype(F32)
    mid = r1.astype(BF16)
    lo = (r1 - mid.astype(F32)).astype(BF16)
    pieces = jnp.concatenate([hi, mid, lo], axis=0)
    outs = []
    for j in range(RW_W // LANES):
        s = _dot(pieces[:, j * LANES:(j + 1) * LANES], bd)
        outs.append(s[:m] + s[m:2 * m] + s[2 * m:])
    return jnp.concatenate(outs, axis=1)


def _head_ones():
    ri = lax.broadcasted_iota(jnp.int32, (LANES, LANES), 0)
    ci = lax.broadcasted_iota(jnp.int32, (LANES, LANES), 1)
    return jnp.where((ri // RW_DH) == (ci // RW_DH), 1.0, 0.0).astype(BF16)


def _rwkv_prep_kernel(*refs, tm, first):
    if first:
        (p_ref, halo_ref, pl_ref, halo_l_ref, vec_ref, w2_ref, a2_ref, g2_ref,
         rd_ref, ad_ref, bh_ref, kh_ref, be_ref, ke_ref, vb_ref, dec_ref, g_ref, bonus_ref, vf_out_ref) = refs
    else:
        (p_ref, halo_ref, pl_ref, halo_l_ref, vf_ref, vec_ref, w2_ref, a2_ref, g2_ref, v2_ref,
         rd_ref, ad_ref, bh_ref, kh_ref, be_ref, ke_ref, vb_ref, dec_ref, g_ref, bonus_ref) = refs
    C = RW_CHUNK
    W = RW_W
    slots = RW_LOW[:3] if first else RW_LOW
    nlr = sum(slots)
    has_prev = (pl.program_id(0) > 0).astype(F32)
    row1 = lax.broadcasted_iota(jnp.int32, (tm, 1), 0)

    def shifted(ref, h_ref, c0, c1):
        prev_last = h_ref[HALO - 1:HALO, c0:c1] * has_prev
        return jnp.where(row1 == 0, prev_last, pltpu.roll(ref[:, c0:c1], 1, axis=0))

    mu_r, mu_k, mu_v, w0, a0, k_k, k_a, r_k, v0 = (vec_ref[i:i + 1, :] for i in range(9))
    r_p, k_p, v_p = (p_ref[:, i * W:(i + 1) * W] for i in range(3))
    r = r_p + (shifted(p_ref, halo_ref, 0, W) - r_p) * mu_r
    k = k_p + (shifted(p_ref, halo_ref, W, 2 * W) - k_p) * mu_k
    v = v_p + (shifted(p_ref, halo_ref, 2 * W, 3 * W) - v_p) * mu_v
    low = pl_ref[:, :nlr] + shifted(pl_ref, halo_l_ref, nlr, 2 * nlr)
    o_w, o_a, o_g, o_v = 0, slots[0], slots[0] + slots[1], slots[0] + slots[1] + slots[2]
    w_log = -_softplus(-(w0 + _dot(jnp.tanh(low[:, o_w:o_a]).astype(BF16), w2_ref[...]))) - 0.5
    lw = -jnp.exp(w_log)
    a = _sigmoid(a0 + _dot(low[:, o_a:o_g].astype(BF16), a2_ref[...]))
    g_ref[...] = _dot(_sigmoid(low[:, o_g:o_v]).astype(BF16), g2_ref[...])
    if first:
        vf_out_ref[...] = v
    else:
        v = v + (vf_ref[...] - v) * _sigmoid(v0 + _dot(low[:, o_v:nlr].astype(BF16), v2_ref[...]))

    bd = _head_ones()
    kk = k * k_k
    kk = kk * lax.rsqrt(jnp.maximum(_head_sum(kk * kk, bd), 1e-24))
    k = k * (1.0 + (a - 1.0) * k_a)
    bonus_ref[...] = _head_sum(r * k * r_k, bd) * v
    b_vec = kk * a

    rr = lax.broadcasted_iota(jnp.int32, (tm, tm), 0)
    cc = lax.broadcasted_iota(jnp.int32, (tm, tm), 1)
    same = (rr // C) == (cc // C)
    cum_incl = jnp.where(same & (cc <= rr), 1.0, 0.0).astype(BF16)
    cum_all = jnp.where(same, 1.0, 0.0).astype(BF16)
    lw_hi, lw_lo = _split(lw)
    g_in = _dot(cum_incl, lw_hi) + _dot(cum_incl, lw_lo)
    g_last = _dot(cum_all, lw_hi) + _dot(cum_all, lw_lo)
    rd_ref[...] = (r * jnp.exp(g_in)).astype(BF16)
    ad_ref[...] = (-kk * jnp.exp(g_in - lw)).astype(BF16)
    inv = jnp.exp(-g_in)
    bh_ref[...] = (b_vec * inv).astype(BF16)
    kh_ref[...] = (k * inv).astype(BF16)
    to_end = jnp.exp(g_last - g_in)
    be_ref[...] = (b_vec * to_end).astype(BF16)
    ke_ref[...] = (k * to_end).astype(BF16)
    vb_ref[...] = v.astype(BF16)
    dec_ref[...] = jnp.exp(g_last)


def rwkv_prep(p_rkv, p_low, v_first, vecs, w2, a2, g2, v2, tm=256):
    T = p_rkv.shape[0]
    tm = min(tm, T)
    first = v_first is None
    row = lambda n: pl.BlockSpec((tm, n), lambda i: (i, 0))
    full = lambda arr: pl.BlockSpec(arr.shape, lambda i: (0, 0))
    halo = lambda n: pl.BlockSpec((HALO, n), lambda i: (jnp.maximum(i * (tm // HALO) - 1, 0), 0))
    w_rkv, w_low = p_rkv.shape[1], p_low.shape[1]
    ins = ([p_rkv, p_rkv, p_low, p_low] + ([] if first else [v_first]) + [vecs, w2, a2, g2]
           + ([] if first else [v2]))
    in_specs = ([row(w_rkv), halo(w_rkv), row(w_low), halo(w_low)] + ([] if first else [row(RW_W)])
                + [full(vecs), full(w2), full(a2), full(g2)] + ([] if first else [full(v2)]))
    n_bf, n_f32 = 7, (4 if first else 3)
    outs = pl.pallas_call(
        functools.partial(_rwkv_prep_kernel, tm=tm, first=first),
        grid=(T // tm,),
        in_specs=in_specs,
        out_specs=[row(RW_W)] * (n_bf + n_f32),
        out_shape=[jax.ShapeDtypeStruct((T, RW_W), BF16)] * n_bf + [jax.ShapeDtypeStruct((T, RW_W), F32)] * n_f32,
        compiler_params=_params("parallel"),
        name="rwkv_prep",
    )(*ins)
    return outs


def _rwkv_post_kernel(o_ref, g_ref, bonus_ref, lng_ref, lnb_ref, y_ref):
    bd = _head_ones()
    o = o_ref[...]
    d = o - _head_sum(o, bd) * (1.0 / RW_DH)
    var = _head_sum(d * d, bd) * (1.0 / RW_DH)
    y = d * lax.rsqrt(var + RW_LN_EPS) * lng_ref[...] + lnb_ref[...] + bonus_ref[...]
    y_ref[...] = (y * g_ref[...]).astype(y_ref.dtype)


def rwkv_post(o, g, bonus, ln_g, ln_b, tm=256):
    T = o.shape[0]
    tm = min(tm, T)
    row = pl.BlockSpec((tm, RW_W), lambda i: (i, 0))
    vec = pl.BlockSpec((1, RW_W), lambda i: (0, 0))
    return pl.pallas_call(
        _rwkv_post_kernel,
        grid=(T // tm,),
        in_specs=[row, row, row, vec, vec],
        out_specs=row,
        out_shape=jax.ShapeDtypeStruct((T, RW_W), BF16),
        compiler_params=_params("parallel"),
        name="rwkv_post",
    )(o, g, bonus, ln_g.reshape(1, RW_W), ln_b.reshape(1, RW_W))


def rwkv_time_mix(p_rkv, p_low, v_first, vecs, w2, a2, g2, v2, ln_g, ln_b):
    outs = rwkv_prep(p_rkv, p_low, v_first, vecs, w2, a2, g2, v2)
    rd, ad, bh, kh, be, ke, vb, dec, g, bonus = outs[:10]
    if v_first is None:
        v_first = outs[10]
    o = wkv7(rd, ad, bh, kh, be, ke, vb, dec)
    return rwkv_post(o, g, bonus, ln_g, ln_b), v_first


def _pad_cols(w, mult):
    n = w.shape[1]
    pad = (-n) % mult
    return w if pad == 0 else jnp.pad(w, ((0, 0), (0, pad)))


def kernel(x, mem, ffn1_pre, ffn1_post, ffn1_w1, ffn1_w3, ffn1_w2, mix_pre, mix_post, w_in, gla_a1, gla_a2, gla_ab, gla_norm, rw_mu_rkv, rw_mu_wag, rw_w0, rw_w1, rw_w2, rw_a0, rw_a1, rw_a2, rw_g1, rw_g2, rw_k_k, rw_k_a, rw_r_k, rw_ln_g, rw_ln_b, rw_mu_vl, rw_v0, rw_v1, rw_v2, branch_w, gate_a, gate_c, gate_b, w_out, xa_pre, xa_post, mem_norm, xa_wq, xa_wk, xa_wv, xa_wo, ffn2_pre, ffn2_post, ffn2_w1, ffn2_w3, ffn2_w2):
    B, T, D = x.shape
    depth = ffn1_pre.shape[0]
    bf = lambda t: t.astype(BF16)
    sb_end = 3 * SB_W
    gla_end = sb_end + 2 * GLA_KW + 2 * GLA_VW
    w_in_b = bf(w_in)
    ffn_w = [bf(t) for t in (ffn1_w1, ffn1_w3, ffn1_w2, ffn2_w1, ffn2_w3, ffn2_w2)]
    gate_c_b, branch_w_b, w_out_b = bf(gate_c), bf(branch_w), bf(w_out)
    xa_w = [bf(t) for t in (xa_wq, xa_wk, xa_wv, xa_wo)]
    sb_col_scale = jnp.concatenate([jnp.full((SB_W,), SB_QSCALE, F32), jnp.ones((sb_end - SB_W,), F32)])
    outs = []
    for bi in range(B):
        xs = x[bi]
        mem_b = mem[bi]
        h = rmsnorm(xs, ffn1_pre[0])
        v_first = None
        for l in range(depth):
            u = ffn_up(h, ffn_w[0], ffn_w[1], l)
            xs, h = mm_resid(u, ffn_w[2], l, xs, ffn1_post[l], mix_pre[l], 0.5)

            p_sb = mm(h, w_in_b, BF16, layer=l, col0=0, ncols=sb_end, col_scale=sb_col_scale)
            p_gla = mm(h, w_in_b, layer=l, col0=sb_end, ncols=gla_end - sb_end)
            p_rkv = mm(h, w_in_b, layer=l, col0=gla_end, ncols=3 * RW_W)
            lows = [(rw_w1[l], rw_mu_wag[l, 0]), (rw_a1[l], rw_mu_wag[l, 1]), (rw_g1[l], rw_mu_wag[l, 2])]
            if l > 0:
                lows.append((rw_v1[l - 1], rw_mu_vl[l - 1]))
            slot = lambda wl, i: jnp.pad(wl, ((0, 0), (0, RW_LOW[i] - wl.shape[1])))
            w_low = jnp.concatenate([slot(wl * (1.0 - m)[:, None], i) for i, (wl, m) in enumerate(lows)]
                                    + [slot(wl * m[:, None], i) for i, (wl, m) in enumerate(lows)], axis=1)
            gla_ap_col = w_low.shape[1]
            w_low = jnp.concatenate([w_low, _pad_cols(gla_a1[l], LANES)], axis=1)
            p_low = mm(h, bf(_pad_cols(w_low, 1024)))
            ha = mm(h, bf(jnp.concatenate([gate_a[l, g] for g in range(gate_a.shape[1])], axis=1)), BF16)

            y_sb = sb_attention(p_sb)
            a2p = jnp.pad(gla_a2[l], ((0, LANES - gla_a2.shape[1]), (0, 0)))
            y_gla = gla(p_gla, p_low, gla_ap_col // LANES, bf(a2p), gla_ab[l], gla_norm[l])
            v0 = rw_v0[l - 1] if l > 0 else jnp.zeros((RW_W,), F32)
            vec_rows = [rw_mu_rkv[l, 0], rw_mu_rkv[l, 1], rw_mu_rkv[l, 2], rw_w0[l], rw_a0[l], rw_k_k[l], rw_k_a[l],
                        rw_r_k[l].reshape(RW_W), v0]
            vecs = jnp.pad(jnp.stack(vec_rows, axis=0), ((0, RW_VEC_ROWS - len(vec_rows)), (0, 0)))
            pad_rows = lambda w2d, i: jnp.pad(w2d, ((0, RW_LOW[i] - w2d.shape[0]), (0, 0)))
            v2 = bf(pad_rows(rw_v2[l - 1], 3)) if l > 0 else None
            y_rw, v_first = rwkv_time_mix(p_rkv, p_low, v_first, vecs, bf(pad_rows(rw_w2[l], 0)),
                                          bf(pad_rows(rw_a2[l], 1)), bf(pad_rows(rw_g2[l], 2)), v2,
                                          rw_ln_g[l], rw_ln_b[l])

            merged = merge(ha, y_sb, y_gla, y_rw, gate_c_b, gate_b[l], branch_w_b, l)
            y = mm(merged, w_out_b, layer=l)
            xs, h = resid_norm(xs, y, mix_post[l], xa_pre[l], 1.0)

            mem_n = rmsnorm(mem_b, mem_norm[l])
            kx = mm(mem_n, xa_w[1], BF16, layer=l)
            vx = mm(mem_n, xa_w[2], BF16, layer=l)
            xs, h = xattn_resid(h, xs, xa_w[0], kx, vx, xa_w[3], l, xa_post[l], ffn2_pre[l])

            u = ffn_up(h, ffn_w[3], ffn_w[4], l)
            g_next = ffn1_pre[l + 1] if l + 1 < depth else ffn1_pre[0]
            xs, h = mm_resid(u, ffn_w[5], l, xs, ffn2_post[l], g_next, 0.5)
        outs.append(xs)
    return jnp.stack(outs, axis=0)
```
